```python
import jax, jax.numpy as jnp
from jax import lax
import numpy as np

D_MODEL = 2048
BATCH = 4
SEQ = 2048
DEPTH = 4
DEC_BATCH = 128
DEC_SEQ = 4
PAST_LEN = 8192
PAGE_SIZE = 128

N_META = 16
BLOCK = 128
N_A = DEPTH // 2
N_B = DEPTH - N_A
MLA_HEADS = 16
Q_LORA = 512
KV_LORA = 512
QK_NOPE = 128
QK_ROPE = 64
V_HEAD = 128
ROPE_THETA = 10000.0
MLA_SCALE = (QK_NOPE + QK_ROPE) ** -0.5
WINDOW = 128
SWA_HEADS = 32
SWA_KV_HEADS = 8
SWA_GROUP = SWA_HEADS // SWA_KV_HEADS
SWA_HEAD_DIM = 64
SWA_SCALE = SWA_HEAD_DIM ** -0.5
ALIBI_MAX_BIAS = 8.0
D_FF = 5632
N_EXPERTS = 8
TOP_K = 2
D_FF_EXPERT = 7168
N_DENSE = (DEPTH + 1) // 2
N_MOE = DEPTH // 2
LN_EPS = 1e-5
RMS_EPS = 1e-6
NEG_INF = -1e30
DEEPNORM_ALPHA = (2 * DEPTH) ** 0.25
DEEPNORM_BETA = (8 * DEPTH) ** -0.25

kernel_name = 'yoco_mla_swa_sink_moe_decoder_step'


def layer_norm(x, g, b):
    xf = x.astype(jnp.float32)
    mu = jnp.mean(xf, -1, keepdims=True)
    var = jnp.mean(jnp.square(xf - mu), -1, keepdims=True)
    return ((xf - mu) * lax.rsqrt(var + LN_EPS) * g.astype(jnp.float32) + b.astype(jnp.float32)).astype(x.dtype)


def rms_norm(x, g):
    xf = x.astype(jnp.float32)
    return (xf * lax.rsqrt(jnp.mean(jnp.square(xf), -1, keepdims=True) + RMS_EPS) * g.astype(jnp.float32)).astype(x.dtype)


def rope_angles(pos):
    inv = ROPE_THETA ** (-jnp.arange(0, QK_ROPE, 2, dtype=jnp.float32) / QK_ROPE)
    return pos.astype(jnp.float32)[..., None] * inv


def apply_rope(x, ang):
    xf = x.astype(jnp.float32).reshape(x.shape[:-1] + (QK_ROPE // 2, 2))
    c, s = jnp.cos(ang), jnp.sin(ang)
    x0, x1 = xf[..., 0], xf[..., 1]
    return jnp.stack([x0 * c - x1 * s, x0 * s + x1 * c], -1).reshape(x.shape).astype(x.dtype)


def swiglu(h, wg, wu, wd):
    return (jax.nn.silu(h @ wg) * (h @ wu)) @ wd


def moe_ffn(h, w_router, we_gate, we_up, we_down):
    logits = (h @ w_router).astype(jnp.float32)
    top_v, top_i = lax.top_k(logits, TOP_K)
    top_w = jax.nn.softmax(top_v, axis=-1)
    gates = jnp.sum(jax.nn.one_hot(top_i, N_EXPERTS, dtype=jnp.float32) * top_w[..., None], -2).astype(h.dtype)
    out = jnp.zeros_like(h)
    for e in range(N_EXPERTS):
        out = out + gates[..., e:e + 1] * swiglu(h, we_gate[e], we_up[e], we_down[e])
    return out


def mla_project(h, pos, w_dq, g_q, w_uq, w_dkv, g_kv, w_uk):
    n, s, _ = h.shape
    cq = rms_norm(h @ w_dq, g_q)
    q = (cq @ w_uq).reshape(n, s, MLA_HEADS, QK_NOPE + QK_ROPE)
    ang = rope_angles(pos)
    q_pe = apply_rope(q[..., QK_NOPE:], ang[..., None, :])
    kv = h @ w_dkv
    ckv = rms_norm(kv[..., :KV_LORA], g_kv)
    kpe = apply_rope(kv[..., KV_LORA:], ang)
    q_lat = jnp.einsum('bshn,rhn->bshr', q[..., :QK_NOPE], w_uk)
    return q_lat, q_pe, ckv, kpe


def mla_attend(q_lat, q_pe, ckv, kpe, q_pos, k_pos, k_valid):
    s = (jnp.einsum('...qhr,...sr->...hqs', q_lat, ckv) + jnp.einsum('...qhp,...sp->...hqs', q_pe, kpe)).astype(jnp.float32) * MLA_SCALE
    mask = k_valid[..., None, None, :] & (k_pos[..., None, None, :] <= q_pos[..., None, :, None])
    p = jax.nn.softmax(jnp.where(mask, s, NEG_INF), axis=-1).astype(ckv.dtype)
    return jnp.einsum('...hqs,...sr->...qhr', p, ckv)


def mla_prompt(q_lat, q_pe, ckv, kpe, pos, valid):
    b, lp = q_lat.shape[:2]
    nb = lp // BLOCK
    qb = q_lat.reshape(b, nb, BLOCK, MLA_HEADS, KV_LORA).swapaxes(0, 1)
    pb = q_pe.reshape(b, nb, BLOCK, MLA_HEADS, QK_ROPE).swapaxes(0, 1)
    posb = pos.reshape(nb, BLOCK)

    def one(args):
        ql, qp, qpos = args
        return mla_attend(ql, qp, ckv, kpe, qpos[None], pos[None], valid[None])

    out = lax.map(one, (qb, pb, posb))
    return out.swapaxes(0, 1).reshape(b, lp, MLA_HEADS, KV_LORA)


def mla_sample(q_lat, q_pe, ckv_new, kpe_new, pool_ckv, pool_kpe, layer, page_table, past_len):
    ds = q_lat.shape[1]
    k_pos = jnp.arange(past_len + ds, dtype=jnp.int32)
    q_pos = past_len + jnp.arange(ds, dtype=jnp.int32)
    k_valid = k_pos >= 0

    def one(args):
        ql, qp, cn, kn, pages = args
        ck = jnp.concatenate([pool_ckv[layer, pages].reshape(-1, KV_LORA), cn], 0)
        kp = jnp.concatenate([pool_kpe[layer, pages].reshape(-1, QK_ROPE), kn], 0)
        return mla_attend(ql, qp, ck, kp, q_pos, k_pos, k_valid)

    return lax.map(one, (q_lat, q_pe, ckv_new, kpe_new, page_table))


def mla_output(o_lat, w_uv, w_o):
    n, s = o_lat.shape[:2]
    o = jnp.einsum('bshr,rhv->bshv', o_lat, w_uv).reshape(n, s, MLA_HEADS * V_HEAD)
    return o @ w_o


def swa_attend(q, k, v, q_pos, k_pos, k_valid, sink, slopes):
    s = jnp.einsum('...qkgd,...skd->...kgqs', q, k).astype(jnp.float32) * SWA_SCALE
    dist = q_pos[..., :, None] - k_pos[..., None, :]
    s = s - slopes.reshape(SWA_KV_HEADS, SWA_GROUP, 1, 1) * dist[..., None, None, :, :].astype(jnp.float32)
    mask = k_valid[..., None, :] & (dist >= 0) & (dist < WINDOW)
    s = jnp.where(mask[..., None, None, :, :], s, NEG_INF)
    sk = sink.astype(jnp.float32).reshape(SWA_KV_HEADS, SWA_GROUP, 1, 1)
    m = jnp.maximum(jnp.max(s, -1, keepdims=True), sk)
    p = jnp.exp(s - m)
    denom = jnp.sum(p, -1, keepdims=True) + jnp.exp(sk - m)
    return jnp.einsum('...kgqs,...skd->...qkgd', (p / denom).astype(v.dtype), v)


def band_blocks(t):
    nb = t.shape[1] // BLOCK
    tb = t.reshape((t.shape[0], nb, BLOCK) + t.shape[2:])
    prev = jnp.concatenate([jnp.zeros_like(tb[:, :1]), tb[:, :-1]], 1)
    return jnp.concatenate([prev, tb], 2)


def swa_prompt(q, k, v, pos, valid, sink, slopes):
    b, lp = q.shape[:2]
    nb = lp // BLOCK
    qb = q.reshape((b, nb, BLOCK) + q.shape[2:])
    o = swa_attend(qb, band_blocks(k), band_blocks(v), pos.reshape(1, nb, BLOCK),
                   band_blocks(pos[None]), band_blocks(valid[None]), sink, slopes)
    return o.reshape(b, lp, SWA_HEADS * SWA_HEAD_DIM)


def setup_inputs(seed: int = 0) -> dict:
    key = jax.random.key(seed)
    ks = iter(jax.random.split(key, 48))

    def nrm(shape, scale=1.0):
        return jax.random.normal(next(ks), shape, jnp.float32) * scale

    n_pages = PAST_LEN // PAGE_SIZE
    n_used = DEC_BATCH * n_pages
    n_pool = n_used + (n_used + 3) // 4
    win_buf = min(WINDOW, PAST_LEN)
    page_table = jax.random.permutation(next(ks), n_pool)[:n_used].reshape(DEC_BATCH, n_pages).astype(jnp.int32)
    d = D_MODEL
    return {
        'x_prompt': nrm((BATCH, SEQ, d)),
        'x_sample': nrm((DEC_BATCH, DEC_SEQ, d)),
        'cache_mla_ckv': nrm((N_A, n_pool, PAGE_SIZE, KV_LORA)),
        'cache_mla_kpe': nrm((N_A, n_pool, PAGE_SIZE, QK_ROPE)),
        'cache_swa_k': nrm((DEC_BATCH, win_buf, SWA_KV_HEADS, SWA_HEAD_DIM)),
        'cache_swa_v': nrm((DEC_BATCH, win_buf, SWA_KV_HEADS, SWA_HEAD_DIM)),
        'page_table': page_table,
        'meta_tokens': nrm((N_META, d)),
        'w_dq': nrm((N_A, d, Q_LORA), d ** -0.5),
        'g_q': 1.0 + nrm((N_A, Q_LORA), 0.01),
        'w_uq': nrm((N_A, Q_LORA, MLA_HEADS * (QK_NOPE + QK_ROPE)), Q_LORA ** -0.5),
        'w_dkv': nrm((N_A, d, KV_LORA + QK_ROPE), d ** -0.5),
        'g_kv': 1.0 + nrm((N_A, KV_LORA), 0.01),
        'w_uk': nrm((N_A, KV_LORA, MLA_HEADS, QK_NOPE), KV_LORA ** -0.5),
        'w_uv': nrm((N_A, KV_LORA, MLA_HEADS, V_HEAD), KV_LORA ** -0.5),
        'w_o_mla': nrm((N_A, MLA_HEADS * V_HEAD, d), (MLA_HEADS * V_HEAD) ** -0.5 * DEEPNORM_BETA),
        'w_k_shared': nrm((d, SWA_KV_HEADS * SWA_HEAD_DIM), d ** -0.5),
        'w_v_shared': nrm((d, SWA_KV_HEADS * SWA_HEAD_DIM), d ** -0.5),
        'w_q_swa': nrm((N_B, d, SWA_HEADS * SWA_HEAD_DIM), d ** -0.5),
        'w_o_swa': nrm((N_B, SWA_HEADS * SWA_HEAD_DIM, d), (SWA_HEADS * SWA_HEAD_DIM) ** -0.5 * DEEPNORM_BETA),
        'sinks': nrm((N_B, SWA_HEADS), 0.5),
        'ln_mix_g': 1.0 + nrm((DEPTH, d), 0.01),
        'ln_mix_b': nrm((DEPTH, d), 0.01),
        'ln_ffn_g': 1.0 + nrm((DEPTH, d), 0.01),
        'ln_ffn_b': nrm((DEPTH, d), 0.01),
        'w_gate': nrm((N_DENSE, d, D_FF), d ** -0.5),
        'w_up': nrm((N_DENSE, d, D_FF), d ** -0.5),
        'w_down': nrm((N_DENSE, D_FF, d), D_FF ** -0.5 * DEEPNORM_BETA),
        'w_router': nrm((N_MOE, d, N_EXPERTS), d ** -0.5),
        'we_gate': nrm((N_MOE, N_EXPERTS, d, D_FF_EXPERT), d ** -0.5),
        'we_up': nrm((N_MOE, N_EXPERTS, d, D_FF_EXPERT), d ** -0.5),
        'we_down': nrm((N_MOE, N_EXPERTS, D_FF_EXPERT, d), D_FF_EXPERT ** -0.5 * DEEPNORM_BETA),
    }


def reference(x_prompt, x_sample, cache_mla_ckv, cache_mla_kpe, cache_swa_k, cache_swa_v, page_table,
              meta_tokens, w_dq, g_q, w_uq, w_dkv, g_kv, w_uk, w_uv, w_o_mla,
              w_k_shared, w_v_shared, w_q_swa, w_o_swa, sinks,
              ln_mix_g, ln_mix_b, ln_ffn_g, ln_ffn_b,
              w_gate, w_up, w_down, w_router, we_gate, we_up, we_down):
    b, seq, d = x_prompt.shape
    db, ds, _ = x_sample.shape
    past_len = page_table.shape[1] * cache_mla_ckv.shape[2]
    win_buf = cache_swa_k.shape[1]
    pad = BLOCK - N_META
    lp = seq + BLOCK
    pos_p = jnp.arange(lp, dtype=jnp.int32) - pad
    valid_p = pos_p >= 0
    pos_s = past_len + jnp.arange(ds, dtype=jnp.int32)
    slopes = 2.0 ** (-ALIBI_MAX_BIAS * jnp.arange(1, SWA_HEADS + 1, dtype=jnp.float32) / SWA_HEADS)

    hp = jnp.concatenate([jnp.zeros((b, pad, d), x_prompt.dtype),
                          jnp.broadcast_to(meta_tokens.astype(x_prompt.dtype)[None], (b, N_META, d)),
                          x_prompt], 1)
    hs = x_sample
    ckv_p, kpe_p, ckv_s, kpe_s = [], [], [], []
    for l in range(DEPTH):
        if l < N_A:
            ql_p, qp_p, c_p, r_p = mla_project(hp, pos_p[None], w_dq[l], g_q[l], w_uq[l], w_dkv[l], g_kv[l], w_uk[l])
            ql_s, qp_s, c_s, r_s = mla_project(hs, pos_s[None], w_dq[l], g_q[l], w_uq[l], w_dkv[l], g_kv[l], w_uk[l])
            mix_p = mla_output(mla_prompt(ql_p, qp_p, c_p, r_p, pos_p, valid_p), w_uv[l], w_o_mla[l])
            mix_s = mla_output(mla_sample(ql_s, qp_s, c_s, r_s, cache_mla_ckv, cache_mla_kpe, l, page_table, past_len),
                               w_uv[l], w_o_mla[l])
            ckv_p.append(c_p[:, pad:])
            kpe_p.append(r_p[:, pad:])
            ckv_s.append(c_s)
            kpe_s.append(r_s)
        else:
            if l == N_A:
                k_p = (hp @ w_k_shared).reshape(b, lp, SWA_KV_HEADS, SWA_HEAD_DIM)
                v_p = (hp @ w_v_shared).reshape(b, lp, SWA_KV_HEADS, SWA_HEAD_DIM)
                k_all = jnp.concatenate([cache_swa_k, (hs @ w_k_shared).reshape(db, ds, SWA_KV_HEADS, SWA_HEAD_DIM)], 1)
                v_all = jnp.concatenate([cache_swa_v, (hs @ w_v_shared).reshape(db, ds, SWA_KV_HEADS, SWA_HEAD_DIM)], 1)
                kpos_s = past_len - win_buf + jnp.arange(win_buf + ds, dtype=jnp.int32)
            j = l - N_A
            q_p = (hp @ w_q_swa[j]).reshape(b, lp, SWA_KV_HEADS, SWA_GROUP, SWA_HEAD_DIM)
            q_s = (hs @ w_q_swa[j]).reshape(db, ds, SWA_KV_HEADS, SWA_GROUP, SWA_HEAD_DIM)
            mix_p = swa_prompt(q_p, k_p, v_p, pos_p, valid_p, sinks[j], slopes) @ w_o_swa[j]
            o_s = swa_attend(q_s, k_all, v_all, pos_s[None], kpos_s[None], (kpos_s >= 0)[None], sinks[j], slopes)
            mix_s = o_s.reshape(db, ds, SWA_HEADS * SWA_HEAD_DIM) @ w_o_swa[j]
        hp = layer_norm(DEEPNORM_ALPHA * hp + mix_p, ln_mix_g[l], ln_mix_b[l])
        hs = layer_norm(DEEPNORM_ALPHA * hs + mix_s, ln_mix_g[l], ln_mix_b[l])
        f = l // 2
        if l % 2 == 0:
            f_p = swiglu(hp, w_gate[f], w_up[f], w_down[f])
            f_s = swiglu(hs, w_gate[f], w_up[f], w_down[f])
        else:
            f_p = moe_ffn(hp, w_router[f], we_gate[f], we_up[f], we_down[f])
            f_s = moe_ffn(hs, w_router[f], we_gate[f], we_up[f], we_down[f])
        hp = layer_norm(DEEPNORM_ALPHA * hp + f_p, ln_ffn_g[l], ln_ffn_b[l])
        hs = layer_norm(DEEPNORM_ALPHA * hs + f_s, ln_ffn_g[l], ln_ffn_b[l])

    y_prompt = hp[:, BLOCK:]
    y_sample = hs
    new_ckv_prompt = jnp.stack(ckv_p)
    new_kpe_prompt = jnp.stack(kpe_p)
    new_ckv_sample = jnp.stack(ckv_s)
    new_kpe_sample = jnp.stack(kpe_s)
    wp = min(WINDOW, seq + N_META)
    new_swa_k_prompt = k_p[:, lp - wp:]
    new_swa_v_prompt = v_p[:, lp - wp:]
    new_swa_k_sample = k_all[:, ds:]
    new_swa_v_sample = v_all[:, ds:]
    return (y_prompt, y_sample, new_ckv_prompt, new_kpe_prompt, new_ckv_sample, new_kpe_sample,
            new_swa_k_prompt, new_swa_v_prompt, new_swa_k_sample, new_swa_v_sample)
```

```python
import functools

import numpy as np
import jax
import jax.numpy as jnp
from jax import lax
from jax.experimental import pallas as pl
from jax.experimental.pallas import tpu as pltpu

F32 = jnp.float32
BF16 = jnp.bfloat16

BLOCK = 128
WINDOW = 128
ROPE_THETA = 10000.0
ALIBI_MAX_BIAS = 8.0
LN_EPS = 1e-5
RMS_EPS = 1e-6
NEG_INF = -1e30
N_EXPERT_LANES = 128

V7X_VMEM_BYTES = 64 * 1024 * 1024
VMEM_CEILING = V7X_VMEM_BYTES - 6 * 1024 * 1024
MIB = 1024 * 1024


def _params(semantics, vmem_mib):
    return pltpu.CompilerParams(dimension_semantics=semantics,
                                vmem_limit_bytes=min(int(vmem_mib * MIB), VMEM_CEILING))


def _nt(a, b):
    return lax.dot_general(a, b, (((1,), (1,)), ((), ())), preferred_element_type=F32)


def _nn(a, b):
    return jnp.dot(a, b, preferred_element_type=F32)


def _rms(x, g):
    return x * lax.rsqrt(jnp.mean(jnp.square(x), -1, keepdims=True) + RMS_EPS) * g


def _ln_store(z, g_ref, b_ref, o_ref, ob_ref):
    mu = jnp.mean(z, -1, keepdims=True)
    d = z - mu
    var = jnp.mean(jnp.square(d), -1, keepdims=True)
    y = d * lax.rsqrt(var + LN_EPS) * g_ref[...] + b_ref[...]
    o_ref[...] = y
    ob_ref[...] = y.astype(ob_ref.dtype)


def _silu(g):
    return g * (1.0 / (1.0 + jnp.exp(-g)))


def _mm_kernel(x_ref, w_ref, o_ref):
    o_ref[...] = _nn(x_ref[...], w_ref[...]).astype(o_ref.dtype)


def matmul_rows(x, w, out_dtype, tm=512):
    m, k = x.shape
    n = w.shape[1]
    return pl.pallas_call(
        _mm_kernel,
        grid=(m // tm,),
        in_specs=[pl.BlockSpec((tm, k), lambda i: (i, 0)),
                  pl.BlockSpec((k, n), lambda i: (0, 0))],
        out_specs=pl.BlockSpec((tm, n), lambda i: (i, 0)),
        out_shape=jax.ShapeDtypeStruct((m, n), out_dtype),
        compiler_params=_params(("parallel",), 40),
        name="matmul_rows",
    )(x, w)


def _mla_down_kernel(x_ref, wq_ref, wkv_ref, wr_ref, gq_ref, gkv_ref, c_ref, s_ref,
                     cq_ref, ckv_ref, ckvb_ref, kpe_ref):
    x = x_ref[...]
    cq_ref[...] = _rms(_nn(x, wq_ref[...]), gq_ref[...]).astype(cq_ref.dtype)
    ckv = _rms(_nn(x, wkv_ref[...]), gkv_ref[...])
    ckv_ref[...] = ckv
    ckvb_ref[...] = ckv.astype(ckvb_ref.dtype)
    r = _nn(x, wr_ref[...])
    half = r.shape[1] // 2
    kpe_ref[...] = r[:, :half] * c_ref[...] + r[:, half:] * s_ref[...]


def mla_down(xb, wq, wkv, wr, gq, gkv, ctab, stab, tm=512):
    n, d = xb.shape
    ql, kl = wq.shape[1], wkv.shape[1]
    row = lambda w: pl.BlockSpec((tm, w), lambda i: (i, 0))
    full = lambda a: pl.BlockSpec(a.shape, lambda i: (0,) * a.ndim)
    return pl.pallas_call(
        _mla_down_kernel,
        grid=(n // tm,),
        in_specs=[row(d), full(wq), full(wkv), full(wr), full(gq), full(gkv), row(128), row(128)],
        out_specs=[row(ql), row(kl), row(kl), row(128)],
        out_shape=[jax.ShapeDtypeStruct((n, ql), BF16), jax.ShapeDtypeStruct((n, kl), F32),
                   jax.ShapeDtypeStruct((n, kl), BF16), jax.ShapeDtypeStruct((n, 128), F32)],
        compiler_params=_params(("parallel",), 40),
        name="mla_down",
    )(xb, wq, wkv, wr, gq, gkv, ctab, stab)


def _mla_qup_kernel(cq_ref, w_ref, c_ref, s_ref, o_ref, *, heads_per_step):
    cq = cq_ref[...]
    c = c_ref[...]
    s = s_ref[...]
    for h in range(heads_per_step):
        acc = _nn(cq, w_ref[:, h * 384:(h + 1) * 384])
        o_ref[:, h * 256:h * 256 + 128] = acc[:, :128].astype(o_ref.dtype)
        o_ref[:, h * 256 + 128:(h + 1) * 256] = (acc[:, 128:256] * c + acc[:, 256:384] * s).astype(o_ref.dtype)


def mla_qup(cq, w, ctab, stab, heads, tm=512, hps=4):
    n, ql = cq.shape
    return pl.pallas_call(
        functools.partial(_mla_qup_kernel, heads_per_step=hps),
        grid=(n // tm, heads // hps),
        in_specs=[pl.BlockSpec((tm, ql), lambda i, j: (i, 0)),
                  pl.BlockSpec((ql, hps * 384), lambda i, j: (0, j)),
                  pl.BlockSpec((tm, 128), lambda i, j: (i, 0)),
                  pl.BlockSpec((tm, 128), lambda i, j: (i, 0))],
        out_specs=pl.BlockSpec((tm, hps * 256), lambda i, j: (i, j)),
        out_shape=jax.ShapeDtypeStruct((n, heads * 256), BF16),
        compiler_params=_params(("parallel", "parallel"), 32),
        name="mla_qup",
    )(cq, w, ctab, stab)


def _mla_kvup_kernel(c_ref, kpe_ref, wk_ref, wv_ref, k_ref, v_ref, *, heads_per_step):
    c = c_ref[...]
    kn = _nn(c, wk_ref[...])
    pe = kpe_ref[...].astype(k_ref.dtype)
    for h in range(heads_per_step):
        k_ref[:, h * 256:h * 256 + 128] = kn[:, h * 128:(h + 1) * 128].astype(k_ref.dtype)
        k_ref[:, h * 256 + 128:(h + 1) * 256] = pe
    v_ref[...] = _nn(c, wv_ref[...]).astype(v_ref.dtype)


def mla_kvup(ckvb, kpe, wk, wv, rows, heads, tm=512, hps=4):
    kl = ckvb.shape[1]
    return pl.pallas_call(
        functools.partial(_mla_kvup_kernel, heads_per_step=hps),
        grid=(rows // tm, heads // hps),
        in_specs=[pl.BlockSpec((tm, kl), lambda i, j: (i, 0)),
                  pl.BlockSpec((tm, 128), lambda i, j: (i, 0)),
                  pl.BlockSpec((kl, hps * 128), lambda i, j: (0, j)),
                  pl.BlockSpec((kl, hps * 128), lambda i, j: (0, j))],
        out_specs=[pl.BlockSpec((tm, hps * 256), lambda i, j: (i, j)),
                   pl.BlockSpec((tm, hps * 128), lambda i, j: (i, j))],
        out_shape=[jax.ShapeDtypeStruct((rows, heads * 256), BF16),
                   jax.ShapeDtypeStruct((rows, heads * 128), BF16)],
        compiler_params=_params(("parallel", "parallel"), 32),
        name="mla_kvup",
    )(ckvb, kpe, wk, wv)


def _mla_prompt_attn_kernel(q_ref, k_ref, v_ref, o_ref, *, blocks, pad, scale):
    for r0, r1 in blocks:
        s = _nt(q_ref[r0:r1, :], k_ref[0:r1, :]) * scale
        row = lax.broadcasted_iota(jnp.int32, s.shape, 0) + r0
        col = lax.broadcasted_iota(jnp.int32, s.shape, 1)
        s = jnp.where(col <= row, jnp.where(col >= pad, s, NEG_INF), NEG_INF)
        m = jnp.max(s, -1, keepdims=True)
        p = jnp.exp(s - m)
        l = jnp.sum(p, -1, keepdims=True)
        o = _nn(p.astype(v_ref.dtype), v_ref[0:r1, :])
        o_ref[r0:r1, :] = (o / l).astype(o_ref.dtype)


def mla_prompt_attn(q, k, v, n_rows, batch, lp, heads, pad, scale):
    blocks = [(0, BLOCK)] + [(r, r + 256) for r in range(BLOCK, lp, 256)]
    assert blocks[-1][1] == lp
    return pl.pallas_call(
        functools.partial(_mla_prompt_attn_kernel, blocks=tuple(blocks), pad=pad, scale=scale),
        grid=(batch, heads),
        in_specs=[pl.BlockSpec((lp, 256), lambda b, h: (b, h)),
                  pl.BlockSpec((lp, 256), lambda b, h: (b, h)),
                  pl.BlockSpec((lp, 128), lambda b, h: (b, h))],
        out_specs=pl.BlockSpec((lp, 128), lambda b, h: (b, h)),
        out_shape=jax.ShapeDtypeStruct((n_rows, heads * 128), BF16),
        compiler_params=_params(("parallel", "parallel"), 40),
        name="mla_prompt_attn",
    )(q, k, v)


def _mla_absorb_kernel(q_ref, w_ref, o_ref):
    o_ref[...] = _nt(q_ref[:, :128], w_ref[...]).astype(o_ref.dtype)


def mla_absorb(q_full, wk, row_block, rows, heads):
    kl = wk.shape[0]
    return pl.pallas_call(
        _mla_absorb_kernel,
        grid=(heads,),
        in_specs=[pl.BlockSpec((rows, 256), lambda h: (row_block, h)),
                  pl.BlockSpec((kl, 128), lambda h: (0, h))],
        out_specs=pl.BlockSpec((rows, kl), lambda h: (0, h)),
        out_shape=jax.ShapeDtypeStruct((rows, heads * kl), BF16),
        compiler_params=_params(("parallel",), 32),
        name="mla_absorb",
    )(q_full, wk)


def _mla_decode_kernel(pt_ref, ql_ref, q_ref, cn_ref, kn_ref, *rest, pages, scale, heads, rope):
    ckv_refs = rest[:pages]
    kpe_refs = rest[pages:2 * pages]
    o_ref, m_sc, l_sc, acc_sc = rest[2 * pages:]
    j = pl.program_id(1)

    @pl.when(j == 0)
    def _():
        m_sc[...] = jnp.full(m_sc.shape, NEG_INF, F32)
        l_sc[...] = jnp.zeros(l_sc.shape, F32)
        acc_sc[...] = jnp.zeros(acc_sc.shape, F32)

    ql = ql_ref[...]
    qp = q_ref[:, 128:128 + rope]

    def update(s, vals):
        m_prev = m_sc[...]
        m_new = jnp.maximum(m_prev, jnp.max(s, -1, keepdims=True))
        a = jnp.exp(m_prev - m_new)
        p = jnp.exp(s - m_new)
        l_sc[...] = a * l_sc[...] + jnp.sum(p, -1, keepdims=True)
        acc_sc[...] = a * acc_sc[...] + _nn(p.astype(vals.dtype), vals)
        m_sc[...] = m_new

    ck = jnp.concatenate([r[...].astype(BF16) for r in ckv_refs], axis=0)
    kp = jnp.concatenate([r[...].astype(BF16) for r in kpe_refs], axis=0)
    update((_nt(ql, ck) + _nt(qp, kp)) * scale, ck)

    @pl.when(j == pl.num_programs(1) - 1)
    def _():
        cn = cn_ref[...]
        kn = kn_ref[:, :rope].astype(BF16)
        s = (_nt(ql, cn) + _nt(qp, kn)) * scale
        tok = lax.broadcasted_iota(jnp.int32, s.shape, 0) // heads
        col = lax.broadcasted_iota(jnp.int32, s.shape, 1)
        update(jnp.where(col <= tok, s, NEG_INF), cn)
        o_ref[...] = (acc_sc[...] / l_sc[...]).astype(o_ref.dtype)


def mla_decode(page_table, q_lat, q_s, ckv_new, kpe_new, pool_ckv, pool_kpe, layer, heads, scale, pages=8):
    db, n_pages = page_table.shape
    page, kl = pool_ckv.shape[2], pool_ckv.shape[3]
    rope = pool_kpe.shape[3]
    rows = q_lat.shape[0] // db
    new_rows = ckv_new.shape[1]
    pt = page_table.reshape(-1)

    def pool_spec(width, i):
        return pl.BlockSpec((None, None, page, width),
                            lambda b, j, pt_ref, i=i: (layer, pt_ref[b * n_pages + j * pages + i], 0, 0))

    in_specs = [pl.BlockSpec((rows, kl), lambda b, j, pt_ref: (b, 0)),
                pl.BlockSpec((rows, 256), lambda b, j, pt_ref: (b, 0)),
                pl.BlockSpec((None, new_rows, kl), lambda b, j, pt_ref: (b, 0, 0)),
                pl.BlockSpec((None, new_rows, 128), lambda b, j, pt_ref: (b, 0, 0))]
    in_specs += [pool_spec(kl, i) for i in range(pages)]
    in_specs += [pool_spec(rope, i) for i in range(pages)]
    return pl.pallas_call(
        functools.partial(_mla_decode_kernel, pages=pages, scale=scale, heads=heads, rope=rope),
        grid_spec=pltpu.PrefetchScalarGridSpec(
            num_scalar_prefetch=1,
            grid=(db, n_pages // pages),
            in_specs=in_specs,
            out_specs=pl.BlockSpec((rows, kl), lambda b, j, pt_ref: (b, 0)),
            scratch_shapes=[pltpu.VMEM((rows, 1), F32), pltpu.VMEM((rows, 1), F32),
                            pltpu.VMEM((rows, kl), F32)]),
        out_shape=jax.ShapeDtypeStruct(q_lat.shape, BF16),
        compiler_params=_params(("parallel", "arbitrary"), 32),
        name="mla_decode",
    )(pt, q_lat, q_s, ckv_new, kpe_new, *([pool_ckv] * pages), *([pool_kpe] * pages))


def mla_vup(o_lat2d, wv, heads):
    rows = o_lat2d.shape[0]
    kl = wv.shape[0]
    return pl.pallas_call(
        _mm_kernel,
        grid=(heads,),
        in_specs=[pl.BlockSpec((rows, kl), lambda h: (0, h)),
                  pl.BlockSpec((kl, 128), lambda h: (0, h))],
        out_specs=pl.BlockSpec((rows, 128), lambda h: (0, h)),
        out_shape=jax.ShapeDtypeStruct((rows, heads * 128), BF16),
        compiler_params=_params(("parallel",), 32),
        name="mla_vup",
    )(o_lat2d, wv)


def _proj_ln_kernel(xp_ref, xs_ref, w_ref, r_ref, g_ref, b_ref, o_ref, ob_ref, *, alpha, prompt_blocks):
    x = jnp.where(pl.program_id(0) < prompt_blocks, xp_ref[...], xs_ref[...])
    z = alpha * r_ref[...] + _nn(x, w_ref[...])
    _ln_store(z, g_ref, b_ref, o_ref, ob_ref)


def proj_ln(x_p, x_s, w, resid, g, b, alpha, tm=256):
    k = x_p.shape[1]
    n, d = resid.shape
    pb = x_p.shape[0] // tm
    row = lambda wd: pl.BlockSpec((tm, wd), lambda i: (i, 0))
    full = lambda a: pl.BlockSpec(a.shape, lambda i: (0,) * a.ndim)
    return pl.pallas_call(
        functools.partial(_proj_ln_kernel, alpha=alpha, prompt_blocks=pb),
        grid=(n // tm,),
        in_specs=[pl.BlockSpec((tm, k), lambda i: (jnp.minimum(i, pb - 1), 0)),
                  pl.BlockSpec((tm, k), lambda i: (jnp.maximum(i - pb, 0), 0)),
                  full(w), row(d), full(g), full(b)],
        out_specs=[row(d), row(d)],
        out_shape=[jax.ShapeDtypeStruct((n, d), F32), jax.ShapeDtypeStruct((n, d), BF16)],
        compiler_params=_params(("parallel",), 48),
        name="proj_ln",
    )(x_p, x_s, w, resid, g, b)


def _ffn_gu_kernel(x_ref, wg_ref, wu_ref, o_ref, wgb, wub):
    @pl.when(pl.program_id(1) == 0)
    def _():
        wgb[...] = wg_ref[...].astype(BF16)
        wub[...] = wu_ref[...].astype(BF16)

    x = x_ref[...]
    g = _nn(x, wgb[...])
    u = _nn(x, wub[...])
    o_ref[...] = (_silu(g) * u).astype(o_ref.dtype)


def ffn_gate_up(xb, w_gate, w_up, f, tm=1024, tn=512):
    n, d = xb.shape
    dff = w_gate.shape[2]
    wspec = pl.BlockSpec((None, d, tn), lambda j, i: (f, 0, j))
    return pl.pallas_call(
        _ffn_gu_kernel,
        grid=(dff // tn, n // tm),
        in_specs=[pl.BlockSpec((tm, d), lambda j, i: (i, 0)), wspec, wspec],
        out_specs=pl.BlockSpec((tm, tn), lambda j, i: (i, j)),
        out_shape=jax.ShapeDtypeStruct((n, dff), BF16),
        scratch_shapes=[pltpu.VMEM((d, tn), BF16), pltpu.VMEM((d, tn), BF16)],
        compiler_params=_params(("arbitrary", "arbitrary"), 48),
        name="ffn_gate_up",
    )(xb, w_gate, w_up)


def _ffn_down_ln_kernel(h_ref, w_ref, r_ref, g_ref, b_ref, o_ref, ob_ref, acc, *, alpha):
    k = pl.program_id(1)

    @pl.when(k == 0)
    def _():
        acc[...] = jnp.zeros(acc.shape, F32)

    acc[...] += _nn(h_ref[...], w_ref[...].astype(BF16))

    @pl.when(k == pl.num_programs(1) - 1)
    def _():
        _ln_store(alpha * r_ref[...] + acc[...], g_ref, b_ref, o_ref, ob_ref)


def ffn_down_ln(h, w_down, f, resid, g, b, alpha, tm=512, tk=512):
    n, dff = h.shape
    d = w_down.shape[2]
    row = pl.BlockSpec((tm, d), lambda i, k: (i, 0))
    vec = lambda a: pl.BlockSpec(a.shape, lambda i, k: (0,) * a.ndim)
    return pl.pallas_call(
        functools.partial(_ffn_down_ln_kernel, alpha=alpha),
        grid=(n // tm, dff // tk),
        in_specs=[pl.BlockSpec((tm, tk), lambda i, k: (i, k)),
                  pl.BlockSpec((None, tk, d), lambda i, k: (f, k, 0)),
                  row, vec(g), vec(b)],
        out_specs=[row, row],
        out_shape=[jax.ShapeDtypeStruct((n, d), F32), jax.ShapeDtypeStruct((n, d), BF16)],
        scratch_shapes=[pltpu.VMEM((tm, d), F32)],
        compiler_params=_params(("parallel", "arbitrary"), 48),
        name="ffn_down_ln",
    )(h, w_down, resid, g, b)


def _router_kernel(x_ref, w_ref, info_ref, cnt_ref, carry, *, n_experts):
    i = pl.program_id(0)

    @pl.when(i == 0)
    def _():
        carry[...] = jnp.zeros(carry.shape, F32)

    logits = jnp.dot(x_ref[...], w_ref[...], preferred_element_type=F32, precision=lax.Precision.HIGHEST)
    tm = logits.shape[0]
    lane = lax.broadcasted_iota(jnp.int32, logits.shape, 1).astype(F32)
    lowest = float(np.finfo(np.float32).min)
    lg = jnp.where(lane < n_experts, logits, lowest)
    v1 = jnp.max(lg, -1, keepdims=True)
    i1 = jnp.min(jnp.where(lg == v1, lane, float(N_EXPERT_LANES)), -1, keepdims=True)
    lg2 = jnp.where(lane == i1, lowest, lg)
    v2 = jnp.max(lg2, -1, keepdims=True)
    i2 = jnp.min(jnp.where(lg2 == v2, lane, float(N_EXPERT_LANES)), -1, keepdims=True)
    e2 = jnp.exp(v2 - v1)
    w1 = 1.0 / (1.0 + e2)
    w2 = e2 / (1.0 + e2)
    sel1 = lane == i1
    sel2 = lane == i2
    onehot = jnp.where(sel1, 1.0, jnp.where(sel2, 1.0, 0.0))
    r = lax.broadcasted_iota(jnp.int32, (tm, tm), 0)
    c = lax.broadcasted_iota(jnp.int32, (tm, tm), 1)
    lower = jnp.where(c < r, 1.0, 0.0).astype(BF16)
    before = _nn(lower, onehot.astype(BF16)) + carry[...]
    rank1 = jnp.sum(jnp.where(sel1, before, 0.0), -1, keepdims=True)
    rank2 = jnp.sum(jnp.where(sel2, before, 0.0), -1, keepdims=True)
    total = carry[...] + jnp.sum(onehot, 0, keepdims=True)
    carry[...] = total
    cols = (i1.astype(F32), i2.astype(F32), rank1, rank2, w1, w2)
    info = jnp.zeros(logits.shape, F32)
    for k, val in enumerate(cols):
        info = jnp.where(lane == k, val, info)
    info_ref[...] = info
    cnt_ref[...] = jnp.broadcast_to(total, cnt_ref.shape)


def moe_router(x, w_pad, n_experts, tm=512):
    n, d = x.shape
    return pl.pallas_call(
        functools.partial(_router_kernel, n_experts=n_experts),
        grid=(n // tm,),
        in_specs=[pl.BlockSpec((tm, d), lambda i: (i, 0)),
                  pl.BlockSpec(w_pad.shape, lambda i: (0, 0))],
        out_specs=[pl.BlockSpec((tm, N_EXPERT_LANES), lambda i: (i, 0)),
                   pl.BlockSpec((8, N_EXPERT_LANES), lambda i: (0, 0))],
        out_shape=[jax.ShapeDtypeStruct((n, N_EXPERT_LANES), F32),
                   jax.ShapeDtypeStruct((8, N_EXPERT_LANES), F32)],
        scratch_shapes=[pltpu.VMEM((1, N_EXPERT_LANES), F32)],
        compiler_params=_params(("arbitrary",), 32),
        name="moe_router",
    )(x, w_pad)


def _row_copy(src_hbm, dst_vmem, src_row, dst_row, sem):
    return pltpu.make_async_copy(src_hbm.at[pl.ds(src_row, 1), :], dst_vmem.at[pl.ds(dst_row, 1), :], sem)


def _moe_gather_kernel(tok_ref, x_hbm, o_ref, buf, sem):
    tm = buf.shape[0]
    base = pl.program_id(0) * tm

    def start(r, carry):
        _row_copy(x_hbm, buf, tok_ref[base + r], r, sem).start()
        return carry

    def wait(r, carry):
        _row_copy(x_hbm, buf, 0, r, sem).wait()
        return carry

    lax.fori_loop(0, tm, start, 0)
    lax.fori_loop(0, tm, wait, 0)
    o_ref[...] = buf[...].astype(o_ref.dtype)


def moe_gather(token_of_row, x, tm=256):
    rows = token_of_row.shape[0]
    d = x.shape[1]
    return pl.pallas_call(
        _moe_gather_kernel,
        grid_spec=pltpu.PrefetchScalarGridSpec(
            num_scalar_prefetch=1,
            grid=(rows // tm,),
            in_specs=[pl.BlockSpec(memory_space=pl.ANY)],
            out_specs=pl.BlockSpec((tm, d), lambda i, tok: (i, 0)),
            scratch_shapes=[pltpu.VMEM((tm, d), F32), pltpu.SemaphoreType.DMA(())]),
        out_shape=jax.ShapeDtypeStruct((rows, d), BF16),
        compiler_params=_params(("arbitrary",), 32),
        name="moe_gather",
    )(token_of_row, x)


def _moe_gu_kernel(te_ref, me_ref, nv_ref, x_ref, wg_ref, wu_ref, o_ref, wgb, wub):
    m = pl.program_id(1)
    valid = m < nv_ref[0]
    first = jnp.logical_or(m == 0, te_ref[m] != te_ref[jnp.maximum(m - 1, 0)])

    @pl.when(jnp.logical_and(valid, first))
    def _():
        wgb[...] = wg_ref[...].astype(BF16)
        wub[...] = wu_ref[...].astype(BF16)

    @pl.when(valid)
    def _():
        x = x_ref[...]
        g = _nn(x, wgb[...])
        u = _nn(x, wub[...])
        o_ref[...] = (_silu(g) * u).astype(o_ref.dtype)

    @pl.when(jnp.logical_not(valid))
    def _():
        o_ref[...] = jnp.zeros(o_ref.shape, o_ref.dtype)


def moe_gate_up(tile_expert, tile_eff, n_valid, xs, we_gate, we_up, f, tm=512, tn=512):
    rows, d = xs.shape
    dff = we_gate.shape[3]
    wspec = pl.BlockSpec((None, None, d, tn), lambda j, i, te, me, nv: (f, te[i], 0, j))
    return pl.pallas_call(
        _moe_gu_kernel,
        grid_spec=pltpu.PrefetchScalarGridSpec(
            num_scalar_prefetch=3,
            grid=(dff // tn, rows // tm),
            in_specs=[pl.BlockSpec((tm, d), lambda j, i, te, me, nv: (me[i], 0)), wspec, wspec],
            out_specs=pl.BlockSpec((tm, tn), lambda j, i, te, me, nv: (i, j)),
            scratch_shapes=[pltpu.VMEM((d, tn), BF16), pltpu.VMEM((d, tn), BF16)]),
        out_shape=jax.ShapeDtypeStruct((rows, dff), BF16),
        compiler_params=_params(("arbitrary", "arbitrary"), 40),
        name="moe_gate_up",
    )(tile_expert, tile_eff, n_valid, xs, we_gate, we_up)


def _moe_down_kernel(te_ref, me_ref, nv_ref, h_ref, w_ref, o_ref, wb):
    m = pl.program_id(1)
    valid = m < nv_ref[0]
    first = jnp.logical_or(m == 0, te_ref[m] != te_ref[jnp.maximum(m - 1, 0)])

    @pl.when(jnp.logical_and(valid, first))
    def _():
        wb[...] = w_ref[...].astype(BF16)

    @pl.when(valid)
    def _():
        o_ref[...] = _nn(h_ref[...], wb[...])

    @pl.when(jnp.logical_not(valid))
    def _():
        o_ref[...] = jnp.zeros(o_ref.shape, o_ref.dtype)


def moe_down(tile_expert, tile_eff, n_valid, h, we_down, f, tm=256, tn=512):
    rows, dff = h.shape
    d = we_down.shape[3]
    return pl.pallas_call(
        _moe_down_kernel,
        grid_spec=pltpu.PrefetchScalarGridSpec(
            num_scalar_prefetch=3,
            grid=(d // tn, rows // tm),
            in_specs=[pl.BlockSpec((tm, dff), lambda j, i, te, me, nv: (me[i], 0)),
                      pl.BlockSpec((None, None, dff, tn), lambda j, i, te, me, nv: (f, te[i], 0, j))],
            out_specs=pl.BlockSpec((tm, tn), lambda j, i, te, me, nv: (i, j)),
            scratch_shapes=[pltpu.VMEM((dff, tn), BF16)]),
        out_shape=jax.ShapeDtypeStruct((rows, d), F32),
        compiler_params=_params(("arbitrary", "arbitrary"), 54),
        name="moe_down",
    )(tile_expert, tile_eff, n_valid, h, we_down)


def _moe_combine_ln_kernel(p1_ref, p2_ref, y_hbm, info_ref, r_ref, g_ref, b_ref, o_ref, ob_ref,
                           buf1, buf2, sem, *, alpha):
    tm = buf1.shape[0]
    base = pl.program_id(0) * tm

    def start(r, carry):
        _row_copy(y_hbm, buf1, p1_ref[base + r], r, sem.at[0]).start()
        _row_copy(y_hbm, buf2, p2_ref[base + r], r, sem.at[1]).start()
        return carry

    def wait(r, carry):
        _row_copy(y_hbm, buf1, 0, r, sem.at[0]).wait()
        _row_copy(y_hbm, buf2, 0, r, sem.at[1]).wait()
        return carry

    lax.fori_loop(0, tm, start, 0)
    lax.fori_loop(0, tm, wait, 0)
    info = info_ref[...]
    f = info[:, 4:5] * buf1[...] + info[:, 5:6] * buf2[...]
    _ln_store(alpha * r_ref[...] + f, g_ref, b_ref, o_ref, ob_ref)


def moe_combine_ln(pos1, pos2, y, info, resid, g, b, alpha, tm=256):
    n, d = resid.shape
    row = lambda w: pl.BlockSpec((tm, w), lambda i, p1, p2: (i, 0))
    vec = lambda a: pl.BlockSpec(a.shape, lambda i, p1, p2: (0,) * a.ndim)
    return pl.pallas_call(
        functools.partial(_moe_combine_ln_kernel, alpha=alpha),
        grid_spec=pltpu.PrefetchScalarGridSpec(
            num_scalar_prefetch=2,
            grid=(n // tm,),
            in_specs=[pl.BlockSpec(memory_space=pl.ANY), row(N_EXPERT_LANES), row(d), vec(g), vec(b)],
            out_specs=[row(d), row(d)],
            scratch_shapes=[pltpu.VMEM((tm, d), F32), pltpu.VMEM((tm, d), F32),
                            pltpu.SemaphoreType.DMA((2,))]),
        out_shape=[jax.ShapeDtypeStruct((n, d), F32), jax.ShapeDtypeStruct((n, d), BF16)],
        compiler_params=_params(("arbitrary",), 40),
        name="moe_combine_ln",
    )(pos1, pos2, y, info, resid, g, b)


def moe_ffn_ln(x, xb, w_router, we_gate, we_up, we_down, f, g, b, alpha, tile=512):
    n, d = x.shape
    n_experts = w_router.shape[2]
    w_pad = jnp.pad(w_router[f], ((0, 0), (0, N_EXPERT_LANES - n_experts)))
    info, cnt = moe_router(x, w_pad, n_experts)
    e1 = info[:, 0].astype(jnp.int32)
    e2 = info[:, 1].astype(jnp.int32)
    counts = cnt[0, :n_experts].astype(jnp.int32)
    group = (counts + tile - 1) // tile * tile
    ends = jnp.cumsum(group)
    starts = ends - group
    pos1 = starts[e1] + info[:, 2].astype(jnp.int32)
    pos2 = starts[e2] + info[:, 3].astype(jnp.int32)
    rows = 2 * n + n_experts * tile
    tok = jnp.arange(n, dtype=jnp.int32)
    token_of_row = jnp.zeros((rows,), jnp.int32).at[pos1].set(tok).at[pos2].set(tok)
    n_tiles = rows // tile
    n_valid = ends[-1] // tile
    tiles = jnp.arange(n_tiles, dtype=jnp.int32)
    tile_eff = jnp.minimum(tiles, n_valid - 1)
    tile_expert = jnp.sum((tile_eff[:, None] >= (ends // tile)[None, :]).astype(jnp.int32), -1)
    nv = n_valid.reshape(1).astype(jnp.int32)

    xs = moe_gather(token_of_row, x)
    h = moe_gate_up(tile_expert, tile_eff, nv, xs, we_gate, we_up, f, tm=tile)
    half = tile // 2
    tiles2 = jnp.arange(2 * n_tiles, dtype=jnp.int32)
    tile_eff2 = jnp.minimum(tiles2, 2 * n_valid - 1)
    y = moe_down(tile_expert[tile_eff2 // 2], tile_eff2, 2 * nv, h, we_down, f, tm=half)
    return moe_combine_ln(pos1, pos2, y, info, x, g, b, alpha)


def _swa_kernel(hs_ref, q_ref, kp_ref, kc_ref, vp_ref, vc_ref, o_ref, *, tq, group, scale, start_fn):
    q_start = start_fn(pl.program_id(0), pl.program_id(1))
    n_pairs = kp_ref.shape[1] // 128
    lane = lax.broadcasted_iota(jnp.int32, (1, 128), 1)
    low = lane < 64
    rows = group * tq
    r = lax.broadcasted_iota(jnp.int32, (rows, 1), 0) % tq
    c_prev = lax.broadcasted_iota(jnp.int32, (1, kp_ref.shape[0]), 1)
    c_cur = lax.broadcasted_iota(jnp.int32, (1, tq), 1)
    dist_prev = r + WINDOW - c_prev
    dist_cur = r - c_cur
    in_window = lambda dist: jnp.logical_and(dist >= 0, dist < WINDOW)
    ok_prev = jnp.logical_and(in_window(dist_prev), (q_start - WINDOW + c_prev) >= 0)
    ok_cur = jnp.logical_and(in_window(dist_cur), (q_start + c_cur) >= 0)
    dprev = dist_prev.astype(F32)
    dcur = dist_cur.astype(F32)
    blk = lax.broadcasted_iota(jnp.int32, (rows, 1), 0) // tq

    for p in range(n_pairs):
        sl = slice(p * 128, (p + 1) * 128)
        kp = kp_ref[:, sl].astype(BF16)
        kc = kc_ref[:, sl].astype(BF16)
        vp = vp_ref[:, sl].astype(BF16)
        vc = vc_ref[:, sl].astype(BF16)
        q4 = jnp.concatenate([q_ref[:, (p * group + i) * 128:(p * group + i + 1) * 128]
                              for i in range(group)], axis=0)
        out = None
        for half in range(2):
            keep = low if half == 0 else jnp.logical_not(low)
            zero = jnp.zeros((), BF16)
            slope = jnp.zeros((rows, 1), F32)
            sink = jnp.zeros((rows, 1), F32)
            for i in range(group):
                head = (2 * p + half) * group + i
                slope = jnp.where(blk == i, hs_ref[0, head], slope)
                sink = jnp.where(blk == i, hs_ref[1, head], sink)
            s_prev = _nt(q4, jnp.where(keep, kp, zero)) * scale - slope * dprev
            s_cur = _nt(q4, jnp.where(keep, kc, zero)) * scale - slope * dcur
            s_prev = jnp.where(ok_prev, s_prev, NEG_INF)
            s_cur = jnp.where(ok_cur, s_cur, NEG_INF)
            m = jnp.maximum(jnp.maximum(jnp.max(s_prev, -1, keepdims=True),
                                        jnp.max(s_cur, -1, keepdims=True)), sink)
            e_prev = jnp.exp(s_prev - m)
            e_cur = jnp.exp(s_cur - m)
            denom = jnp.sum(e_prev, -1, keepdims=True) + jnp.sum(e_cur, -1, keepdims=True) + jnp.exp(sink - m)
            o = (_nn((e_prev / denom).astype(BF16), jnp.where(keep, vp, zero))
                 + _nn((e_cur / denom).astype(BF16), jnp.where(keep, vc, zero)))
            out = o if out is None else out + o
        for i in range(group):
            o_ref[:, (p * group + i) * 128:(p * group + i + 1) * 128] = out[i * tq:(i + 1) * tq].astype(o_ref.dtype)


def swa_attn_prompt(hs, q, kv, attn_rows, batch, lp, pad, group, scale):
    nb = lp // BLOCK
    kvw = kv.shape[1] // 2
    d = q.shape[1]
    smem = pl.BlockSpec(memory_space=pltpu.SMEM)
    return pl.pallas_call(
        functools.partial(_swa_kernel, tq=BLOCK, group=group, scale=scale,
                          start_fn=lambda b, j: j * BLOCK - pad),
        grid=(batch, nb),
        in_specs=[smem,
                  pl.BlockSpec((BLOCK, d), lambda b, j: (b * nb + j, 0)),
                  pl.BlockSpec((BLOCK, kvw), lambda b, j: (b * nb + jnp.maximum(j - 1, 0), 0)),
                  pl.BlockSpec((BLOCK, kvw), lambda b, j: (b * nb + j, 0)),
                  pl.BlockSpec((BLOCK, kvw), lambda b, j: (b * nb + jnp.maximum(j - 1, 0), 1)),
                  pl.BlockSpec((BLOCK, kvw), lambda b, j: (b * nb + j, 1))],
        out_specs=pl.BlockSpec((BLOCK, d), lambda b, j: (b * nb + j, 0)),
        out_shape=jax.ShapeDtypeStruct((attn_rows, d), BF16),
        compiler_params=_params(("parallel", "parallel"), 32),
        name="swa_attn_prompt",
    )(hs, q, kv, kv, kv, kv)


def swa_attn_sample(hs, q_s, k_cache, v_cache, k_new, v_new, past_len, group, scale):
    db, tq, d = q_s.shape
    kvw = k_cache.shape[2]
    smem = pl.BlockSpec(memory_space=pltpu.SMEM)
    seq = lambda r, w: pl.BlockSpec((None, r, w), lambda b, j: (b, 0, 0))
    return pl.pallas_call(
        functools.partial(_swa_kernel, tq=tq, group=group, scale=scale, start_fn=lambda b, j: past_len),
        grid=(db, 1),
        in_specs=[smem, seq(tq, d), seq(WINDOW, kvw), seq(tq, kvw), seq(WINDOW, kvw), seq(tq, kvw)],
        out_specs=seq(tq, d),
        out_shape=jax.ShapeDtypeStruct((db, tq, d), BF16),
        compiler_params=_params(("parallel", "arbitrary"), 32),
        name="swa_attn_sample",
    )(hs, q_s, k_cache, k_new, v_cache, v_new)


def _pair_swap(w):
    return w.reshape(w.shape[:-1] + (w.shape[-1] // 2, 2))[..., ::-1].reshape(w.shape)


def _swa_slab_perm(heads, kv_heads, hd):
    group = heads // kv_heads
    cols = []
    for p in range(kv_heads // 2):
        for i in range(group):
            for half in range(2):
                head = (2 * p + half) * group + i
                cols.extend(range(head * hd, (head + 1) * hd))
    return np.asarray(cols, np.int32)


def kernel(x_prompt, x_sample, cache_mla_ckv, cache_mla_kpe, cache_swa_k, cache_swa_v, page_table, meta_tokens, w_dq, g_q, w_uq, w_dkv, g_kv, w_uk, w_uv, w_o_mla, w_k_shared, w_v_shared, w_q_swa, w_o_swa, sinks, ln_mix_g, ln_mix_b, ln_ffn_g, ln_ffn_b, w_gate, w_up, w_down, w_router, we_gate, we_up, we_down):
    b, seq, d = x_prompt.shape
    db, ds, _ = x_sample.shape
    depth = ln_mix_g.shape[0]
    n_a = w_dq.shape[0]
    n_meta = meta_tokens.shape[0]
    kv_lora, heads, qk_nope = w_uk.shape[1:]
    qk_rope = cache_mla_kpe.shape[3]
    v_head = w_uv.shape[3]
    swa_kv_heads, swa_hd = cache_swa_k.shape[2:]
    swa_heads = w_q_swa.shape[2] // swa_hd
    swa_group = swa_heads // swa_kv_heads
    win_buf = cache_swa_k.shape[1]
    past_len = page_table.shape[1] * cache_mla_ckv.shape[2]
    pad = BLOCK - n_meta
    lp = seq + BLOCK
    n_p = b * lp
    n_s = db * ds
    n = n_p + n_s
    alpha = (2 * depth) ** 0.25
    mla_scale = (qk_nope + qk_rope) ** -0.5
    swa_scale = swa_hd ** -0.5
    assert (qk_nope, qk_rope, v_head, swa_hd, win_buf) == (128, 64, 128, 64, WINDOW)
    assert n_p % 512 == 0 and n_s == 512 and n % 1024 == 0
    sample_block = n_p // n_s
    ds_pad = 16

    hp = jnp.concatenate([jnp.zeros((b, pad, d), x_prompt.dtype),
                          jnp.broadcast_to(meta_tokens.astype(x_prompt.dtype)[None], (b, n_meta, d)),
                          x_prompt], 1)
    x = jnp.concatenate([hp.reshape(n_p, d), x_sample.reshape(n_s, d)], 0)
    xb = x.astype(BF16)

    pos = jnp.concatenate([jnp.tile(jnp.arange(lp, dtype=jnp.int32) - pad, b),
                           jnp.tile(past_len + jnp.arange(ds, dtype=jnp.int32), db)])
    inv = ROPE_THETA ** (-jnp.arange(0, qk_rope, 2, dtype=F32) / qk_rope)
    ang = pos.astype(F32)[:, None] * inv
    cos, sin = jnp.cos(ang), jnp.sin(ang)
    ctab = jnp.pad(jnp.repeat(cos, 2, axis=-1), ((0, 0), (0, 128 - qk_rope)))
    stab = jnp.pad(jnp.stack([-sin, sin], -1).reshape(n, qk_rope), ((0, 0), (0, 128 - qk_rope)))

    slopes = 2.0 ** (-ALIBI_MAX_BIAS * jnp.arange(1, swa_heads + 1, dtype=F32) / swa_heads)
    perm = _swa_slab_perm(swa_heads, swa_kv_heads, swa_hd)

    ckv_out, kpe_out = [], []
    kv = None
    for l in range(depth):
        g_mix, b_mix = ln_mix_g[l][None], ln_mix_b[l][None]
        if l < n_a:
            w_rope = w_dkv[l][:, kv_lora:]
            w_rope_sw = _pair_swap(w_rope)
            wr = jnp.concatenate([w_rope, w_rope_sw, w_rope_sw, w_rope], -1).astype(BF16)
            wq3 = w_uq[l].reshape(-1, heads, qk_nope + qk_rope)
            wq_rope = wq3[..., qk_nope:]
            wq_rope_sw = _pair_swap(wq_rope)
            wq2 = jnp.concatenate([wq3[..., :qk_nope], wq_rope, wq_rope_sw, wq_rope_sw, wq_rope], -1)
            wq2 = wq2.reshape(-1, heads * 384).astype(BF16)
            wk2 = w_uk[l].reshape(kv_lora, heads * qk_nope).astype(BF16)
            wv2 = w_uv[l].reshape(kv_lora, heads * v_head).astype(BF16)

            cq, ckv, ckvb, kpe = mla_down(xb, w_dq[l].astype(BF16), w_dkv[l][:, :kv_lora].astype(BF16), wr,
                                          g_q[l][None], g_kv[l][None], ctab, stab)
            q_full = mla_qup(cq, wq2, ctab, stab, heads)
            k_full, v_full = mla_kvup(ckvb, kpe, wk2, wv2, n_p, heads)
            attn_p = mla_prompt_attn(q_full, k_full, v_full, n_p, b, lp, heads, pad, mla_scale)

            q_lat = mla_absorb(q_full, wk2, sample_block, n_s, heads).reshape(n_s * heads, kv_lora)
            q_s = q_full[n_p:].reshape(n_s * heads, 256)
            ckv_new = jnp.pad(ckvb[n_p:].reshape(db, ds, kv_lora), ((0, 0), (0, ds_pad - ds), (0, 0)))
            kpe_new = jnp.pad(kpe[n_p:].reshape(db, ds, 128), ((0, 0), (0, ds_pad - ds), (0, 0)))
            o_lat = mla_decode(page_table, q_lat, q_s, ckv_new, kpe_new, cache_mla_ckv, cache_mla_kpe,
                               l, heads, mla_scale)
            attn_s = mla_vup(o_lat.reshape(n_s, heads * kv_lora), wv2, heads)
            x, xb = proj_ln(attn_p, attn_s, w_o_mla[l].astype(BF16), x, g_mix, b_mix, alpha)
            ckv_out.append(ckv)
            kpe_out.append(kpe[:, :qk_rope])
        else:
            j = l - n_a
            if kv is None:
                w_kv = jnp.concatenate([w_k_shared, w_v_shared], -1).astype(BF16)
                kv = matmul_rows(xb, w_kv, F32)
                kvw = swa_kv_heads * swa_hd
                k_new = kv[n_p:, :kvw].reshape(db, ds, kvw)
                v_new = kv[n_p:, kvw:].reshape(db, ds, kvw)
                k_new_pad = jnp.pad(k_new, ((0, 0), (0, ds_pad - ds), (0, 0)))
                v_new_pad = jnp.pad(v_new, ((0, 0), (0, ds_pad - ds), (0, 0)))
            q = matmul_rows(xb, w_q_swa[j][:, perm].astype(BF16), BF16)
            hs = jnp.stack([slopes, sinks[j].astype(F32)])
            attn_p = swa_attn_prompt(hs, q, kv, n_p, b, lp, pad, swa_group, swa_scale)
            q_s = jnp.pad(q[n_p:].reshape(db, ds, -1), ((0, 0), (0, ds_pad - ds), (0, 0)))
            attn_s = swa_attn_sample(hs, q_s, cache_swa_k.reshape(db, win_buf, kvw),
                                     cache_swa_v.reshape(db, win_buf, kvw), k_new_pad, v_new_pad,
                                     past_len, swa_group, swa_scale)
            x, xb = proj_ln(attn_p, attn_s[:, :ds].reshape(n_s, -1), w_o_swa[j][perm, :].astype(BF16),
                            x, g_mix, b_mix, alpha)

        g_ffn, b_ffn = ln_ffn_g[l][None], ln_ffn_b[l][None]
        f = l // 2
        if l % 2 == 0:
            h = ffn_gate_up(xb, w_gate, w_up, f)
            x, xb = ffn_down_ln(h, w_down, f, x, g_ffn, b_ffn, alpha)
        else:
            x, xb = moe_ffn_ln(x, xb, w_router, we_gate, we_up, we_down, f, g_ffn, b_ffn, alpha)

    hp_out = x[:n_p].reshape(b, lp, d)
    y_prompt = hp_out[:, BLOCK:]
    y_sample = x[n_p:].reshape(db, ds, d)
    new_ckv_prompt = jnp.stack([c[:n_p].reshape(b, lp, kv_lora)[:, pad:] for c in ckv_out])
    new_kpe_prompt = jnp.stack([c[:n_p].reshape(b, lp, qk_rope)[:, pad:] for c in kpe_out])
    new_ckv_sample = jnp.stack([c[n_p:].reshape(db, ds, kv_lora) for c in ckv_out])
    new_kpe_sample = jnp.stack([c[n_p:].reshape(db, ds, qk_rope) for c in kpe_out])
    wp = min(WINDOW, seq + n_meta)
    k_p = kv[:n_p, :kvw].reshape(b, lp, swa_kv_heads, swa_hd)
    v_p = kv[:n_p, kvw:].reshape(b, lp, swa_kv_heads, swa_hd)
    new_swa_k_prompt = k_p[:, lp - wp:]
    new_swa_v_prompt = v_p[:, lp - wp:]
    new_swa_k_sample = jnp.concatenate([cache_swa_k, k_new.reshape(db, ds, swa_kv_heads, swa_hd)], 1)[:, ds:]
    new_swa_v_sample = jnp.concatenate([cache_swa_v, v_new.reshape(db, ds, swa_kv_heads, swa_hd)], 1)[:, ds:]
    return (y_prompt, y_sample, new_ckv_prompt, new_kpe_prompt, new_ckv_sample, new_kpe_sample,
            new_swa_k_prompt, new_swa_v_prompt, new_swa_k_sample, new_swa_v_sample)
```

```python
import functools

import numpy as np
import jax
import jax.numpy as jnp
from jax import lax
from jax.experimental import pallas as pl
from jax.experimental.pallas import tpu as pltpu

F32 = jnp.float32
BF16 = jnp.bfloat16

BLOCK = 128
WINDOW = 128
ROPE_THETA = 10000.0
ALIBI_MAX_BIAS = 8.0
LN_EPS = 1e-5
RMS_EPS = 1e-6
NEG_INF = -1e30
N_EXPERT_LANES = 128

V7X_VMEM_BYTES = 64 * 1024 * 1024
VMEM_CEILING = V7X_VMEM_BYTES - 6 * 1024 * 1024
MIB = 1024 * 1024


def _params(semantics, vmem_mib):
    return pltpu.CompilerParams(dimension_semantics=semantics,
                                vmem_limit_bytes=min(int(vmem_mib * MIB), VMEM_CEILING))


def _nt(a, b):
    return lax.dot_general(a, b, (((1,), (1,)), ((), ())), preferred_element_type=F32)


def _nn(a, b):
    return jnp.dot(a, b, preferred_element_type=F32)


def _rms(x, g):
    return x * lax.rsqrt(jnp.mean(jnp.square(x), -1, keepdims=True) + RMS_EPS) * g


def _ln_store(z, g_ref, b_ref, o_ref, ob_ref):
    mu = jnp.mean(z, -1, keepdims=True)
    d = z - mu
    var = jnp.mean(jnp.square(d), -1, keepdims=True)
    y = d * lax.rsqrt(var + LN_EPS) * g_ref[...] + b_ref[...]
    o_ref[...] = y
    ob_ref[...] = y.astype(ob_ref.dtype)


def _silu(g):
    return g * (1.0 / (1.0 + jnp.exp(-g)))


def _mm_kernel(x_ref, w_ref, o_ref):
    o_ref[...] = _nn(x_ref[...], w_ref[...]).astype(o_ref.dtype)


def matmul_rows(x, w, out_dtype, tm=512):
    m, k = x.shape
    n = w.shape[1]
    return pl.pallas_call(
        _mm_kernel,
        grid=(m // tm,),
        in_specs=[pl.BlockSpec((tm, k), lambda i: (i, 0)),
                  pl.BlockSpec((k, n), lambda i: (0, 0))],
        out_specs=pl.BlockSpec((tm, n), lambda i: (i, 0)),
        out_shape=jax.ShapeDtypeStruct((m, n), out_dtype),
        compiler_params=_params(("parallel",), 40),
        name="matmul_rows",
    )(x, w)


def _mla_down_kernel(x_ref, wq_ref, wkv_ref, wr_ref, gq_ref, gkv_ref, c_ref, s_ref,
                     cq_ref, ckv_ref, ckvb_ref, kpe_ref):
    x = x_ref[...]
    cq_ref[...] = _rms(_nn(x, wq_ref[...]), gq_ref[...]).astype(cq_ref.dtype)
    ckv = _rms(_nn(x, wkv_ref[...]), gkv_ref[...])
    ckv_ref[...] = ckv
    ckvb_ref[...] = ckv.astype(ckvb_ref.dtype)
    r = _nn(x, wr_ref[...])
    half = r.shape[1] // 2
    kpe_ref[...] = r[:, :half] * c_ref[...] + r[:, half:] * s_ref[...]


def mla_down(xb, wq, wkv, wr, gq, gkv, ctab, stab, tm=512):
    n, d = xb.shape
    ql, kl = wq.shape[1], wkv.shape[1]
    row = lambda w: pl.BlockSpec((tm, w), lambda i: (i, 0))
    full = lambda a: pl.BlockSpec(a.shape, lambda i: (0,) * a.ndim)
    return pl.pallas_call(
        _mla_down_kernel,
        grid=(n // tm,),
        in_specs=[row(d), full(wq), full(wkv), full(wr), full(gq), full(gkv), row(128), row(128)],
        out_specs=[row(ql), row(kl), row(kl), row(128)],
        out_shape=[jax.ShapeDtypeStruct((n, ql), BF16), jax.ShapeDtypeStruct((n, kl), F32),
                   jax.ShapeDtypeStruct((n, kl), BF16), jax.ShapeDtypeStruct((n, 128), F32)],
        compiler_params=_params(("parallel",), 40),
        name="mla_down",
    )(xb, wq, wkv, wr, gq, gkv, ctab, stab)


def _mla_qup_kernel(cq_ref, w_ref, c_ref, s_ref, o_ref, *, heads_per_step):
    cq = cq_ref[...]
    c = c_ref[...]
    s = s_ref[...]
    for h in range(heads_per_step):
        acc = _nn(cq, w_ref[:, h * 384:(h + 1) * 384])
        o_ref[:, h * 256:h * 256 + 128] = acc[:, :128].astype(o_ref.dtype)
        o_ref[:, h * 256 + 128:(h + 1) * 256] = (acc[:, 128:256] * c + acc[:, 256:384] * s).astype(o_ref.dtype)


def mla_qup(cq, w, ctab, stab, heads, tm=512, hps=4):
    n, ql = cq.shape
    return pl.pallas_call(
        functools.partial(_mla_qup_kernel, heads_per_step=hps),
        grid=(n // tm, heads // hps),
        in_specs=[pl.BlockSpec((tm, ql), lambda i, j: (i, 0)),
                  pl.BlockSpec((ql, hps * 384), lambda i, j: (0, j)),
                  pl.BlockSpec((tm, 128), lambda i, j: (i, 0)),
                  pl.BlockSpec((tm, 128), lambda i, j: (i, 0))],
        out_specs=pl.BlockSpec((tm, hps * 256), lambda i, j: (i, j)),
        out_shape=jax.ShapeDtypeStruct((n, heads * 256), BF16),
        compiler_params=_params(("parallel", "parallel"), 32),
        name="mla_qup",
    )(cq, w, ctab, stab)


def _mla_kvup_kernel(c_ref, kpe_ref, wk_ref, wv_ref, k_ref, v_ref, *, heads_per_step):
    c = c_ref[...]
    kn = _nn(c, wk_ref[...])
    pe = kpe_ref[...].astype(k_ref.dtype)
    for h in range(heads_per_step):
        k_ref[:, h * 256:h * 256 + 128] = kn[:, h * 128:(h + 1) * 128].astype(k_ref.dtype)
        k_ref[:, h * 256 + 128:(h + 1) * 256] = pe
    v_ref[...] = _nn(c, wv_ref[...]).astype(v_ref.dtype)


def mla_kvup(ckvb, kpe, wk, wv, rows, heads, tm=512, hps=4):
    kl = ckvb.shape[1]
    return pl.pallas_call(
        functools.partial(_mla_kvup_kernel, heads_per_step=hps),
        grid=(rows // tm, heads // hps),
        in_specs=[pl.BlockSpec((tm, kl), lambda i, j: (i, 0)),
                  pl.BlockSpec((tm, 128), lambda i, j: (i, 0)),
                  pl.BlockSpec((kl, hps * 128), lambda i, j: (0, j)),
                  pl.BlockSpec((kl, hps * 128), lambda i, j: (0, j))],
        out_specs=[pl.BlockSpec((tm, hps * 256), lambda i, j: (i, j)),
                   pl.BlockSpec((tm, hps * 128), lambda i, j: (i, j))],
        out_shape=[jax.ShapeDtypeStruct((rows, heads * 256), BF16),
                   jax.ShapeDtypeStruct((rows, heads * 128), BF16)],
        compiler_params=_params(("parallel", "parallel"), 32),
        name="mla_kvup",
    )(ckvb, kpe, wk, wv)


def _mla_prompt_attn_kernel(q_ref, k_ref, v_ref, o_ref, *, blocks, pad, scale):
    for r0, r1 in blocks:
        s = _nt(q_ref[r0:r1, :], k_ref[0:r1, :]) * scale
        row = lax.broadcasted_iota(jnp.int32, s.shape, 0) + r0
        col = lax.broadcasted_iota(jnp.int32, s.shape, 1)
        s = jnp.where(col <= row, jnp.where(col >= pad, s, NEG_INF), NEG_INF)
        m = jnp.max(s, -1, keepdims=True)
        p = jnp.exp(s - m)
        l = jnp.sum(p, -1, keepdims=True)
        o = _nn(p.astype(v_ref.dtype), v_ref[0:r1, :])
        o_ref[r0:r1, :] = (o / l).astype(o_ref.dtype)


def mla_prompt_attn(q, k, v, n_rows, batch, lp, heads, pad, scale):
    blocks = [(0, BLOCK)] + [(r, r + 256) for r in range(BLOCK, lp, 256)]
    assert blocks[-1][1] == lp
    return pl.pallas_call(
        functools.partial(_mla_prompt_attn_kernel, blocks=tuple(blocks), pad=pad, scale=scale),
        grid=(batch, heads),
        in_specs=[pl.BlockSpec((lp, 256), lambda b, h: (b, h)),
                  pl.BlockSpec((lp, 256), lambda b, h: (b, h)),
                  pl.BlockSpec((lp, 128), lambda b, h: (b, h))],
        out_specs=pl.BlockSpec((lp, 128), lambda b, h: (b, h)),
        out_shape=jax.ShapeDtypeStruct((n_rows, heads * 128), BF16),
        compiler_params=_params(("parallel", "parallel"), 40),
        name="mla_prompt_attn",
    )(q, k, v)


def _mla_absorb_kernel(q_ref, w_ref, o_ref):
    o_ref[...] = _nt(q_ref[:, :128], w_ref[...]).astype(o_ref.dtype)


def mla_absorb(q_full, wk, row_block, rows, heads):
    kl = wk.shape[0]
    return pl.pallas_call(
        _mla_absorb_kernel,
        grid=(heads,),
        in_specs=[pl.BlockSpec((rows, 256), lambda h: (row_block, h)),
                  pl.BlockSpec((kl, 128), lambda h: (0, h))],
        out_specs=pl.BlockSpec((rows, kl), lambda h: (0, h)),
        out_shape=jax.ShapeDtypeStruct((rows, heads * kl), BF16),
        compiler_params=_params(("parallel",), 32),
        name="mla_absorb",
    )(q_full, wk)


def _mla_decode_kernel(pt_ref, ql_ref, q_ref, cn_ref, kn_ref, ckv_hbm, kpe_hbm, o_ref,
                       ckv_buf, kpe_buf, sem, m_sc, l_sc, acc_sc, *, layer, pages, scale, heads, rope):
    j = pl.program_id(1)
    n_chunks = pl.num_programs(1)
    step = pl.program_id(0) * n_chunks + j
    page = ckv_hbm.shape[2]

    def page_copies(page_id, slot, i):
        return (pltpu.make_async_copy(ckv_hbm.at[layer, page_id], ckv_buf.at[slot, pl.ds(i * page, page), :],
                                      sem.at[0, slot]),
                pltpu.make_async_copy(kpe_hbm.at[layer, page_id], kpe_buf.at[slot, i], sem.at[1, slot]))

    def start_chunk(chunk, slot):
        for i in range(pages):
            for copy in page_copies(pt_ref[chunk * pages + i], slot, i):
                copy.start()

    @pl.when(step == 0)
    def _():
        start_chunk(0, 0)

    @pl.when(step + 1 < pl.num_programs(0) * n_chunks)
    def _():
        start_chunk(step + 1, (step + 1) % 2)

    slot = step % 2
    for i in range(pages):
        for copy in page_copies(0, slot, i):
            copy.wait()

    @pl.when(j == 0)
    def _():
        m_sc[...] = jnp.full(m_sc.shape, NEG_INF, F32)
        l_sc[...] = jnp.zeros(l_sc.shape, F32)
        acc_sc[...] = jnp.zeros(acc_sc.shape, F32)

    ql = ql_ref[...]
    qp = q_ref[:, 128:128 + rope]

    def update(s, vals):
        m_prev = m_sc[...]
        m_new = jnp.maximum(m_prev, jnp.max(s, -1, keepdims=True))
        a = jnp.exp(m_prev - m_new)
        p = jnp.exp(s - m_new)
        l_sc[...] = a * l_sc[...] + jnp.sum(p, -1, keepdims=True)
        acc_sc[...] = a * acc_sc[...] + _nn(p.astype(vals.dtype), vals)
        m_sc[...] = m_new

    ck = ckv_buf[slot].astype(BF16)
    kpt = jnp.concatenate([kpe_buf[slot, i].astype(BF16) for i in range(pages)], axis=1)
    update((_nt(ql, ck) + _nn(qp, kpt)) * scale, ck)

    @pl.when(j == pl.num_programs(1) - 1)
    def _():
        cn = cn_ref[...]
        kn = kn_ref[:, :rope].astype(BF16)
        s = (_nt(ql, cn) + _nt(qp, kn)) * scale
        tok = lax.broadcasted_iota(jnp.int32, s.shape, 0) // heads
        col = lax.broadcasted_iota(jnp.int32, s.shape, 1)
        update(jnp.where(col <= tok, s, NEG_INF), cn)
        o_ref[...] = (acc_sc[...] / l_sc[...]).astype(o_ref.dtype)


def mla_decode(page_table, q_lat, q_s, ckv_new, kpe_new, pool_ckv, pool_kpe_t, layer, heads, scale, pages=32):
    db, n_pages = page_table.shape
    page, kl = pool_ckv.shape[2], pool_ckv.shape[3]
    rope = pool_kpe_t.shape[2]
    rows = q_lat.shape[0] // db
    new_rows = ckv_new.shape[1]
    pages = min(pages, n_pages)
    assert n_pages % pages == 0
    pt = page_table.reshape(-1)
    hbm = pl.BlockSpec(memory_space=pl.ANY)
    in_specs = [pl.BlockSpec((rows, kl), lambda b, j, pt_ref: (b, 0)),
                pl.BlockSpec((rows, 256), lambda b, j, pt_ref: (b, 0)),
                pl.BlockSpec((None, new_rows, kl), lambda b, j, pt_ref: (b, 0, 0)),
                pl.BlockSpec((None, new_rows, 128), lambda b, j, pt_ref: (b, 0, 0)),
                hbm, hbm]
    return pl.pallas_call(
        functools.partial(_mla_decode_kernel, layer=layer, pages=pages, scale=scale, heads=heads, rope=rope),
        grid_spec=pltpu.PrefetchScalarGridSpec(
            num_scalar_prefetch=1,
            grid=(db, n_pages // pages),
            in_specs=in_specs,
            out_specs=pl.BlockSpec((rows, kl), lambda b, j, pt_ref: (b, 0)),
            scratch_shapes=[pltpu.VMEM((2, pages * page, kl), F32), pltpu.VMEM((2, pages, rope, page), F32),
                            pltpu.SemaphoreType.DMA((2, 2)),
                            pltpu.VMEM((rows, 1), F32), pltpu.VMEM((rows, 1), F32),
                            pltpu.VMEM((rows, kl), F32)]),
        out_shape=jax.ShapeDtypeStruct(q_lat.shape, BF16),
        compiler_params=_params(("arbitrary", "arbitrary"), 48),
        name="mla_decode",
    )(pt, q_lat, q_s, ckv_new, kpe_new, pool_ckv, pool_kpe_t)


def mla_vup(o_lat2d, wv, heads):
    rows = o_lat2d.shape[0]
    kl = wv.shape[0]
    return pl.pallas_call(
        _mm_kernel,
        grid=(heads,),
        in_specs=[pl.BlockSpec((rows, kl), lambda h: (0, h)),
                  pl.BlockSpec((kl, 128), lambda h: (0, h))],
        out_specs=pl.BlockSpec((rows, 128), lambda h: (0, h)),
        out_shape=jax.ShapeDtypeStruct((rows, heads * 128), BF16),
        compiler_params=_params(("parallel",), 32),
        name="mla_vup",
    )(o_lat2d, wv)


def _proj_ln_kernel(xp_ref, xs_ref, w_ref, r_ref, g_ref, b_ref, o_ref, ob_ref, *, alpha, prompt_blocks):
    x = jnp.where(pl.program_id(0) < prompt_blocks, xp_ref[...], xs_ref[...])
    z = alpha * r_ref[...] + _nn(x, w_ref[...])
    _ln_store(z, g_ref, b_ref, o_ref, ob_ref)


def proj_ln(x_p, x_s, w, resid, g, b, alpha, tm=256):
    k = x_p.shape[1]
    n, d = resid.shape
    pb = x_p.shape[0] // tm
    row = lambda wd: pl.BlockSpec((tm, wd), lambda i: (i, 0))
    full = lambda a: pl.BlockSpec(a.shape, lambda i: (0,) * a.ndim)
    return pl.pallas_call(
        functools.partial(_proj_ln_kernel, alpha=alpha, prompt_blocks=pb),
        grid=(n // tm,),
        in_specs=[pl.BlockSpec((tm, k), lambda i: (jnp.minimum(i, pb - 1), 0)),
                  pl.BlockSpec((tm, k), lambda i: (jnp.maximum(i - pb, 0), 0)),
                  full(w), row(d), full(g), full(b)],
        out_specs=[row(d), row(d)],
        out_shape=[jax.ShapeDtypeStruct((n, d), F32), jax.ShapeDtypeStruct((n, d), BF16)],
        compiler_params=_params(("parallel",), 48),
        name="proj_ln",
    )(x_p, x_s, w, resid, g, b)


def _ffn_gu_kernel(x_ref, wg_ref, wu_ref, o_ref, wgb, wub):
    @pl.when(pl.program_id(1) == 0)
    def _():
        wgb[...] = wg_ref[...].astype(BF16)
        wub[...] = wu_ref[...].astype(BF16)

    x = x_ref[...]
    g = _nn(x, wgb[...])
    u = _nn(x, wub[...])
    o_ref[...] = (_silu(g) * u).astype(o_ref.dtype)


def ffn_gate_up(xb, w_gate, w_up, f, tm=1024, tn=512):
    n, d = xb.shape
    dff = w_gate.shape[2]
    wspec = pl.BlockSpec((None, d, tn), lambda j, i: (f, 0, j))
    return pl.pallas_call(
        _ffn_gu_kernel,
        grid=(dff // tn, n // tm),
        in_specs=[pl.BlockSpec((tm, d), lambda j, i: (i, 0)), wspec, wspec],
        out_specs=pl.BlockSpec((tm, tn), lambda j, i: (i, j)),
        out_shape=jax.ShapeDtypeStruct((n, dff), BF16),
        scratch_shapes=[pltpu.VMEM((d, tn), BF16), pltpu.VMEM((d, tn), BF16)],
        compiler_params=_params(("arbitrary", "arbitrary"), 48),
        name="ffn_gate_up",
    )(xb, w_gate, w_up)


def _ffn_down_ln_kernel(h_ref, w_ref, r_ref, g_ref, b_ref, o_ref, ob_ref, acc, *, alpha):
    k = pl.program_id(1)

    @pl.when(k == 0)
    def _():
        acc[...] = jnp.zeros(acc.shape, F32)

    acc[...] += _nn(h_ref[...], w_ref[...].astype(BF16))

    @pl.when(k == pl.num_programs(1) - 1)
    def _():
        _ln_store(alpha * r_ref[...] + acc[...], g_ref, b_ref, o_ref, ob_ref)


def ffn_down_ln(h, w_down, f, resid, g, b, alpha, tm=1024, tk=512):
    n, dff = h.shape
    d = w_down.shape[2]
    row = pl.BlockSpec((tm, d), lambda i, k: (i, 0), pipeline_mode=pl.Buffered(1))
    vec = lambda a: pl.BlockSpec(a.shape, lambda i, k: (0,) * a.ndim)
    return pl.pallas_call(
        functools.partial(_ffn_down_ln_kernel, alpha=alpha),
        grid=(n // tm, dff // tk),
        in_specs=[pl.BlockSpec((tm, tk), lambda i, k: (i, k)),
                  pl.BlockSpec((None, tk, d), lambda i, k: (f, k, 0)),
                  row, vec(g), vec(b)],
        out_specs=[row, row],
        out_shape=[jax.ShapeDtypeStruct((n, d), F32), jax.ShapeDtypeStruct((n, d), BF16)],
        scratch_shapes=[pltpu.VMEM((tm, d), F32)],
        compiler_params=_params(("parallel", "arbitrary"), 56),
        name="ffn_down_ln",
    )(h, w_down, resid, g, b)


def _router_kernel(x_ref, w_ref, info_ref, cnt_ref, x3_ref, carry, *, n_experts):
    i = pl.program_id(0)

    @pl.when(i == 0)
    def _():
        carry[...] = jnp.zeros(carry.shape, F32)

    logits = jnp.dot(x_ref[...], w_ref[...], preferred_element_type=F32, precision=lax.Precision.HIGHEST)
    tm = logits.shape[0]
    lane = lax.broadcasted_iota(jnp.int32, logits.shape, 1).astype(F32)
    lowest = float(np.finfo(np.float32).min)
    lg = jnp.where(lane < n_experts, logits, lowest)
    v1 = jnp.max(lg, -1, keepdims=True)
    i1 = jnp.min(jnp.where(lg == v1, lane, float(N_EXPERT_LANES)), -1, keepdims=True)
    lg2 = jnp.where(lane == i1, lowest, lg)
    v2 = jnp.max(lg2, -1, keepdims=True)
    i2 = jnp.min(jnp.where(lg2 == v2, lane, float(N_EXPERT_LANES)), -1, keepdims=True)
    e2 = jnp.exp(v2 - v1)
    w1 = 1.0 / (1.0 + e2)
    w2 = e2 / (1.0 + e2)
    sel1 = lane == i1
    sel2 = lane == i2
    onehot = jnp.where(sel1, 1.0, jnp.where(sel2, 1.0, 0.0))
    r = lax.broadcasted_iota(jnp.int32, (tm, tm), 0)
    c = lax.broadcasted_iota(jnp.int32, (tm, tm), 1)
    lower = jnp.where(c < r, 1.0, 0.0).astype(BF16)
    before = _nn(lower, onehot.astype(BF16)) + carry[...]
    rank1 = jnp.sum(jnp.where(sel1, before, 0.0), -1, keepdims=True)
    rank2 = jnp.sum(jnp.where(sel2, before, 0.0), -1, keepdims=True)
    total = carry[...] + jnp.sum(onehot, 0, keepdims=True)
    carry[...] = total
    cols = (i1, i2, rank1, rank2, w1, w2)
    info = jnp.zeros(logits.shape, F32)
    for k, val in enumerate(cols):
        info = jnp.where(lane == k, val, info)
    info_ref[...] = info
    cnt_ref[...] = jnp.broadcast_to(total, cnt_ref.shape)
    chunks = x_ref.shape[1] // 128
    for j in range(chunks):
        x3_ref[pl.ds(j, tm, stride=chunks), :] = x_ref[:, j * 128:(j + 1) * 128]


def moe_router(x, w_pad, n_experts, tm=512):
    n, d = x.shape
    chunks = d // 128
    return pl.pallas_call(
        functools.partial(_router_kernel, n_experts=n_experts),
        grid=(n // tm,),
        in_specs=[pl.BlockSpec((tm, d), lambda i: (i, 0)),
                  pl.BlockSpec(w_pad.shape, lambda i: (0, 0))],
        out_specs=[pl.BlockSpec((tm, N_EXPERT_LANES), lambda i: (i, 0)),
                   pl.BlockSpec((8, N_EXPERT_LANES), lambda i: (0, 0)),
                   pl.BlockSpec((tm * chunks, 128), lambda i: (i, 0))],
        out_shape=[jax.ShapeDtypeStruct((n, N_EXPERT_LANES), F32),
                   jax.ShapeDtypeStruct((8, N_EXPERT_LANES), F32),
                   jax.ShapeDtypeStruct((n * chunks, 128), F32)],
        scratch_shapes=[pltpu.VMEM((1, N_EXPERT_LANES), F32)],
        compiler_params=_params(("arbitrary",), 40),
        name="moe_router",
    )(x, w_pad)


def _row_copy(src_hbm, dst_vmem, src_row, dst_row, sem):
    return pltpu.make_async_copy(src_hbm.at[pl.ds(src_row, 1), :], dst_vmem.at[pl.ds(dst_row, 1), :], sem)


def _moe_gather_kernel(tok_ref, x3_hbm, o_ref, buf, sem):
    tm = o_ref.shape[0]
    chunks = x3_hbm.shape[1]
    i = pl.program_id(0)

    def row_copy(token, slot, r):
        return pltpu.make_async_copy(x3_hbm.at[token], buf.at[slot, pl.ds(r * chunks, chunks), :], sem.at[slot])

    def start_tile(tile, slot):
        def body(r, carry):
            row_copy(tok_ref[tile * tm + r], slot, r).start()
            return carry
        lax.fori_loop(0, tm, body, 0, unroll=8)

    @pl.when(i == 0)
    def _():
        start_tile(0, 0)

    @pl.when(i + 1 < pl.num_programs(0))
    def _():
        start_tile(i + 1, (i + 1) % 2)

    slot = i % 2

    def wait(r, carry):
        row_copy(0, slot, r).wait()
        return carry

    lax.fori_loop(0, tm, wait, 0, unroll=8)
    for j in range(chunks):
        o_ref[:, j * 128:(j + 1) * 128] = buf[slot, pl.ds(j, tm, stride=chunks), :].astype(o_ref.dtype)


def moe_gather(token_of_row, x3, tm=256):
    rows = token_of_row.shape[0]
    chunks = x3.shape[1]
    return pl.pallas_call(
        _moe_gather_kernel,
        grid_spec=pltpu.PrefetchScalarGridSpec(
            num_scalar_prefetch=1,
            grid=(rows // tm,),
            in_specs=[pl.BlockSpec(memory_space=pl.ANY)],
            out_specs=pl.BlockSpec((tm, chunks * 128), lambda i, tok: (i, 0)),
            scratch_shapes=[pltpu.VMEM((2, tm * chunks, 128), F32), pltpu.SemaphoreType.DMA((2,))]),
        out_shape=jax.ShapeDtypeStruct((rows, chunks * 128), BF16),
        compiler_params=_params(("arbitrary",), 32),
        name="moe_gather",
    )(token_of_row, x3)


def _moe_gu_kernel(te_ref, me_ref, nv_ref, x_ref, wg_ref, wu_ref, o_ref, wgb, wub):
    m = pl.program_id(1)
    valid = m < nv_ref[0]
    first = jnp.logical_or(m == 0, te_ref[m] != te_ref[jnp.maximum(m - 1, 0)])

    @pl.when(jnp.logical_and(valid, first))
    def _():
        wgb[...] = wg_ref[...].astype(BF16)
        wub[...] = wu_ref[...].astype(BF16)

    @pl.when(valid)
    def _():
        x = x_ref[...]
        g = _nn(x, wgb[...])
        u = _nn(x, wub[...])
        o_ref[...] = (_silu(g) * u).astype(o_ref.dtype)

    @pl.when(jnp.logical_not(valid))
    def _():
        o_ref[...] = jnp.zeros(o_ref.shape, o_ref.dtype)


def moe_gate_up(tile_expert, tile_eff, n_valid, xs, we_gate, we_up, f, tm=512, tn=1024):
    rows, d = xs.shape
    dff = we_gate.shape[3]
    tn = min(tn, dff)
    wspec = pl.BlockSpec((None, None, d, tn), lambda j, i, te, me, nv: (f, te[i], 0, j))
    return pl.pallas_call(
        _moe_gu_kernel,
        grid_spec=pltpu.PrefetchScalarGridSpec(
            num_scalar_prefetch=3,
            grid=(dff // tn, rows // tm),
            in_specs=[pl.BlockSpec((tm, d), lambda j, i, te, me, nv: (me[i], 0)), wspec, wspec],
            out_specs=pl.BlockSpec((tm, tn), lambda j, i, te, me, nv: (i, j)),
            scratch_shapes=[pltpu.VMEM((d, tn), BF16), pltpu.VMEM((d, tn), BF16)]),
        out_shape=jax.ShapeDtypeStruct((rows, dff), BF16),
        compiler_params=_params(("arbitrary", "arbitrary"), 58),
        name="moe_gate_up",
    )(tile_expert, tile_eff, n_valid, xs, we_gate, we_up)


def _moe_down_kernel(te_ref, me_ref, nv_ref, h_ref, w_ref, o_ref, wb):
    m = pl.program_id(1)
    valid = m < nv_ref[0]
    first = jnp.logical_or(m == 0, te_ref[m] != te_ref[jnp.maximum(m - 1, 0)])

    @pl.when(jnp.logical_and(valid, first))
    def _():
        wb[...] = w_ref[...].astype(BF16)

    @pl.when(valid)
    def _():
        o_ref[...] = _nn(h_ref[...], wb[...])

    @pl.when(jnp.logical_not(valid))
    def _():
        o_ref[...] = jnp.zeros(o_ref.shape, o_ref.dtype)


def moe_down(tile_expert, tile_eff, n_valid, h, we_down, f, tm=256, tn=512):
    rows, dff = h.shape
    d = we_down.shape[3]
    return pl.pallas_call(
        _moe_down_kernel,
        grid_spec=pltpu.PrefetchScalarGridSpec(
            num_scalar_prefetch=3,
            grid=(d // tn, rows // tm),
            in_specs=[pl.BlockSpec((tm, dff), lambda j, i, te, me, nv: (me[i], 0)),
                      pl.BlockSpec((None, None, dff, tn), lambda j, i, te, me, nv: (f, te[i], 0, j))],
            out_specs=pl.BlockSpec((tm, tn), lambda j, i, te, me, nv: (i, j)),
            scratch_shapes=[pltpu.VMEM((dff, tn), BF16)]),
        out_shape=jax.ShapeDtypeStruct((rows, d), F32),
        compiler_params=_params(("arbitrary", "arbitrary"), 54),
        name="moe_down",
    )(tile_expert, tile_eff, n_valid, h, we_down)


def _moe_combine_ln_kernel(p1_ref, p2_ref, y_hbm, info_ref, r_ref, g_ref, b_ref, o_ref, ob_ref,
                           buf1, buf2, sem, *, alpha):
    tm = buf1.shape[0]
    base = pl.program_id(0) * tm

    def start(r, carry):
        _row_copy(y_hbm, buf1, p1_ref[base + r], r, sem.at[0]).start()
        _row_copy(y_hbm, buf2, p2_ref[base + r], r, sem.at[1]).start()
        return carry

    def wait(r, carry):
        _row_copy(y_hbm, buf1, 0, r, sem.at[0]).wait()
        _row_copy(y_hbm, buf2, 0, r, sem.at[1]).wait()
        return carry

    lax.fori_loop(0, tm, start, 0)
    lax.fori_loop(0, tm, wait, 0)
    info = info_ref[...]
    f = info[:, 4:5] * buf1[...] + info[:, 5:6] * buf2[...]
    _ln_store(alpha * r_ref[...] + f, g_ref, b_ref, o_ref, ob_ref)


def moe_combine_ln(pos1, pos2, y, info, resid, g, b, alpha, tm=256):
    n, d = resid.shape
    row = lambda w: pl.BlockSpec((tm, w), lambda i, p1, p2: (i, 0))
    vec = lambda a: pl.BlockSpec(a.shape, lambda i, p1, p2: (0,) * a.ndim)
    return pl.pallas_call(
        functools.partial(_moe_combine_ln_kernel, alpha=alpha),
        grid_spec=pltpu.PrefetchScalarGridSpec(
            num_scalar_prefetch=2,
            grid=(n // tm,),
            in_specs=[pl.BlockSpec(memory_space=pl.ANY), row(N_EXPERT_LANES), row(d), vec(g), vec(b)],
            out_specs=[row(d), row(d)],
            scratch_shapes=[pltpu.VMEM((tm, d), F32), pltpu.VMEM((tm, d), F32),
                            pltpu.SemaphoreType.DMA((2,))]),
        out_shape=[jax.ShapeDtypeStruct((n, d), F32), jax.ShapeDtypeStruct((n, d), BF16)],
        compiler_params=_params(("arbitrary",), 40),
        name="moe_combine_ln",
    )(pos1, pos2, y, info, resid, g, b)


def moe_ffn_ln(x, w_router, we_gate, we_up, we_down, f, g, b, alpha, tile=512):
    n, d = x.shape
    n_experts = w_router.shape[2]
    w_pad = jnp.pad(w_router[f], ((0, 0), (0, N_EXPERT_LANES - n_experts)))
    info, cnt, x3 = moe_router(x, w_pad, n_experts)
    e1 = info[:, 0].astype(jnp.int32)
    e2 = info[:, 1].astype(jnp.int32)
    counts = cnt[0, :n_experts].astype(jnp.int32)
    group = (counts + tile - 1) // tile * tile
    ends = jnp.cumsum(group)
    starts = ends - group
    pos1 = starts[e1] + info[:, 2].astype(jnp.int32)
    pos2 = starts[e2] + info[:, 3].astype(jnp.int32)
    rows = 2 * n + n_experts * tile
    tok = jnp.arange(n, dtype=jnp.int32)
    token_of_row = jnp.zeros((rows,), jnp.int32).at[jnp.concatenate([pos1, pos2])].set(
        jnp.concatenate([tok, tok]), unique_indices=True)
    n_tiles = rows // tile
    n_valid = ends[-1] // tile
    tiles = jnp.arange(n_tiles, dtype=jnp.int32)
    tile_eff = jnp.minimum(tiles, n_valid - 1)
    tile_expert = jnp.sum((tile_eff[:, None] >= (ends // tile)[None, :]).astype(jnp.int32), -1)
    nv = n_valid.reshape(1).astype(jnp.int32)

    xs = moe_gather(token_of_row, x3.reshape(n, d // 128, 128))
    h = moe_gate_up(tile_expert, tile_eff, nv, xs, we_gate, we_up, f, tm=tile)
    half = tile // 2
    tiles2 = jnp.arange(2 * n_tiles, dtype=jnp.int32)
    tile_eff2 = jnp.minimum(tiles2, 2 * n_valid - 1)
    y = moe_down(tile_expert[tile_eff2 // 2], tile_eff2, 2 * nv, h, we_down, f, tm=half)
    return moe_combine_ln(pos1, pos2, y, info, x, g, b, alpha)


def _swa_kernel(hs_ref, q_ref, kp_ref, kc_ref, vp_ref, vc_ref, o_ref, *, tq, group, scale, start_fn):
    q_start = start_fn(pl.program_id(0), pl.program_id(1))
    n_pairs = kp_ref.shape[1] // 128
    lane = lax.broadcasted_iota(jnp.int32, (1, 128), 1)
    low = lane < 64
    rows = group * tq
    r = lax.broadcasted_iota(jnp.int32, (rows, 1), 0) % tq
    c_prev = lax.broadcasted_iota(jnp.int32, (1, kp_ref.shape[0]), 1)
    c_cur = lax.broadcasted_iota(jnp.int32, (1, tq), 1)
    dist_prev = r + WINDOW - c_prev
    dist_cur = r - c_cur
    in_window = lambda dist: jnp.logical_and(dist >= 0, dist < WINDOW)
    ok_prev = jnp.logical_and(in_window(dist_prev), (q_start - WINDOW + c_prev) >= 0)
    ok_cur = jnp.logical_and(in_window(dist_cur), (q_start + c_cur) >= 0)
    dprev = dist_prev.astype(F32)
    dcur = dist_cur.astype(F32)
    blk = lax.broadcasted_iota(jnp.int32, (rows, 1), 0) // tq

    for p in range(n_pairs):
        sl = slice(p * 128, (p + 1) * 128)
        kp = kp_ref[:, sl].astype(BF16)
        kc = kc_ref[:, sl].astype(BF16)
        vp = vp_ref[:, sl].astype(BF16)
        vc = vc_ref[:, sl].astype(BF16)
        q4 = jnp.concatenate([q_ref[:, (p * group + i) * 128:(p * group + i + 1) * 128]
                              for i in range(group)], axis=0)
        out = None
        for half in range(2):
            keep = low if half == 0 else jnp.logical_not(low)
            zero = jnp.zeros((), BF16)
            slope = jnp.zeros((rows, 1), F32)
            sink = jnp.zeros((rows, 1), F32)
            for i in range(group):
                head = (2 * p + half) * group + i
                slope = jnp.where(blk == i, hs_ref[0, head], slope)
                sink = jnp.where(blk == i, hs_ref[1, head], sink)
            s_prev = _nt(q4, jnp.where(keep, kp, zero)) * scale - slope * dprev
            s_cur = _nt(q4, jnp.where(keep, kc, zero)) * scale - slope * dcur
            s_prev = jnp.where(ok_prev, s_prev, NEG_INF)
            s_cur = jnp.where(ok_cur, s_cur, NEG_INF)
            m = jnp.maximum(jnp.maximum(jnp.max(s_prev, -1, keepdims=True),
                                        jnp.max(s_cur, -1, keepdims=True)), sink)
            e_prev = jnp.exp(s_prev - m)
            e_cur = jnp.exp(s_cur - m)
            denom = jnp.sum(e_prev, -1, keepdims=True) + jnp.sum(e_cur, -1, keepdims=True) + jnp.exp(sink - m)
            o = (_nn((e_prev / denom).astype(BF16), jnp.where(keep, vp, zero))
                 + _nn((e_cur / denom).astype(BF16), jnp.where(keep, vc, zero)))
            out = o if out is None else out + o
        for i in range(group):
            o_ref[:, (p * group + i) * 128:(p * group + i + 1) * 128] = out[i * tq:(i + 1) * tq].astype(o_ref.dtype)


def swa_attn_prompt(hs, q, kv, attn_rows, batch, lp, pad, group, scale):
    nb = lp // BLOCK
    kvw = kv.shape[1] // 2
    d = q.shape[1]
    smem = pl.BlockSpec(memory_space=pltpu.SMEM)
    return pl.pallas_call(
        functools.partial(_swa_kernel, tq=BLOCK, group=group, scale=scale,
                          start_fn=lambda b, j: j * BLOCK - pad),
        grid=(batch, nb),
        in_specs=[smem,
                  pl.BlockSpec((BLOCK, d), lambda b, j: (b * nb + j, 0)),
                  pl.BlockSpec((BLOCK, kvw), lambda b, j: (b * nb + jnp.maximum(j - 1, 0), 0)),
                  pl.BlockSpec((BLOCK, kvw), lambda b, j: (b * nb + j, 0)),
                  pl.BlockSpec((BLOCK, kvw), lambda b, j: (b * nb + jnp.maximum(j - 1, 0), 1)),
                  pl.BlockSpec((BLOCK, kvw), lambda b, j: (b * nb + j, 1))],
        out_specs=pl.BlockSpec((BLOCK, d), lambda b, j: (b * nb + j, 0)),
        out_shape=jax.ShapeDtypeStruct((attn_rows, d), BF16),
        compiler_params=_params(("parallel", "parallel"), 32),
        name="swa_attn_prompt",
    )(hs, q, kv, kv, kv, kv)


def _swa_sample_kernel(hrow_ref, q_ref, kp_ref, kc_ref, vp_ref, vc_ref, o_ref, *, group, scale, past_len):
    seqs, tq, d = q_ref.shape
    n_slabs = d // 128
    n_pairs = kp_ref.shape[2] // 128
    rows = 2 * n_slabs * tq
    lane = lax.broadcasted_iota(jnp.int32, (1, 128), 1)
    low = lane < 64
    t = lax.broadcasted_iota(jnp.int32, (rows, 1), 0) % tq
    c_prev = lax.broadcasted_iota(jnp.int32, (1, WINDOW), 1)
    c_cur = lax.broadcasted_iota(jnp.int32, (1, tq), 1)
    dist_prev = t + WINDOW - c_prev
    dist_cur = t - c_cur
    in_window = lambda dist: jnp.logical_and(dist >= 0, dist < WINDOW)
    ok_prev = jnp.logical_and(in_window(dist_prev), (past_len - WINDOW + c_prev) >= 0)
    ok_cur = in_window(dist_cur)
    slope = hrow_ref[0]
    sink = hrow_ref[1][:, :1]
    bias_prev = slope * dist_prev.astype(F32)
    bias_cur = slope[:, :tq] * dist_cur.astype(F32)
    zero_slab = jnp.zeros((tq, 128), BF16)
    zero = jnp.zeros((), BF16)

    for s in range(seqs):
        blocks = []
        for sidx in range(n_slabs):
            p = sidx // group
            slab = q_ref[s, :, sidx * 128:(sidx + 1) * 128]
            for half in range(2):
                own = jnp.where(low, slab, zero) if half == 0 else jnp.where(low, zero, slab)
                blocks.append(jnp.concatenate([zero_slab] * p + [own] + [zero_slab] * (n_pairs - 1 - p), axis=1))
        qexp = jnp.concatenate(blocks, axis=0)
        kp = kp_ref[s].astype(BF16)
        kc = kc_ref[s].astype(BF16)
        s_prev = jnp.where(ok_prev, _nt(qexp, kp) * scale - bias_prev, NEG_INF)
        s_cur = jnp.where(ok_cur, _nt(qexp, kc) * scale - bias_cur, NEG_INF)
        m = jnp.maximum(jnp.maximum(jnp.max(s_prev, -1, keepdims=True), jnp.max(s_cur, -1, keepdims=True)), sink)
        e_prev = jnp.exp(s_prev - m)
        e_cur = jnp.exp(s_cur - m)
        denom = jnp.sum(e_prev, -1, keepdims=True) + jnp.sum(e_cur, -1, keepdims=True) + jnp.exp(sink - m)
        o = (_nn((e_prev / denom).astype(BF16), vp_ref[s].astype(BF16))
             + _nn((e_cur / denom).astype(BF16), vc_ref[s].astype(BF16)))
        for sidx in range(n_slabs):
            p = sidx // group
            r0 = 2 * sidx * tq
            lo = o[r0:r0 + tq, p * 128:(p + 1) * 128]
            hi = o[r0 + tq:r0 + 2 * tq, p * 128:(p + 1) * 128]
            o_ref[s, :, sidx * 128:(sidx + 1) * 128] = jnp.where(low, lo, hi).astype(o_ref.dtype)


def swa_attn_sample(hrow, q_s, k_cache, v_cache, k_new, v_new, past_len, group, scale, seqs=4):
    db, tq, d = q_s.shape
    kvw = k_cache.shape[2]
    seq = lambda r, w: pl.BlockSpec((seqs, r, w), lambda b: (b, 0, 0))
    return pl.pallas_call(
        functools.partial(_swa_sample_kernel, group=group, scale=scale, past_len=past_len),
        grid=(db // seqs,),
        in_specs=[pl.BlockSpec(hrow.shape, lambda b: (0, 0, 0)),
                  seq(tq, d), seq(WINDOW, kvw), seq(tq, kvw), seq(WINDOW, kvw), seq(tq, kvw)],
        out_specs=seq(tq, d),
        out_shape=jax.ShapeDtypeStruct((db, tq, d), BF16),
        compiler_params=_params(("parallel",), 32),
        name="swa_attn_sample",
    )(hrow, q_s, k_cache, k_new, v_cache, v_new)


def _pair_swap(w):
    return w.reshape(w.shape[:-1] + (w.shape[-1] // 2, 2))[..., ::-1].reshape(w.shape)


def _swa_slab_perm(heads, kv_heads, hd):
    group = heads // kv_heads
    cols = []
    for p in range(kv_heads // 2):
        for i in range(group):
            for half in range(2):
                head = (2 * p + half) * group + i
                cols.extend(range(head * hd, (head + 1) * hd))
    return np.asarray(cols, np.int32)


def kernel(x_prompt, x_sample, cache_mla_ckv, cache_mla_kpe, cache_swa_k, cache_swa_v, page_table, meta_tokens, w_dq, g_q, w_uq, w_dkv, g_kv, w_uk, w_uv, w_o_mla, w_k_shared, w_v_shared, w_q_swa, w_o_swa, sinks, ln_mix_g, ln_mix_b, ln_ffn_g, ln_ffn_b, w_gate, w_up, w_down, w_router, we_gate, we_up, we_down):
    b, seq, d = x_prompt.shape
    db, ds, _ = x_sample.shape
    depth = ln_mix_g.shape[0]
    n_a = w_dq.shape[0]
    n_meta = meta_tokens.shape[0]
    kv_lora, heads, qk_nope = w_uk.shape[1:]
    qk_rope = cache_mla_kpe.shape[3]
    v_head = w_uv.shape[3]
    swa_kv_heads, swa_hd = cache_swa_k.shape[2:]
    swa_heads = w_q_swa.shape[2] // swa_hd
    swa_group = swa_heads // swa_kv_heads
    win_buf = cache_swa_k.shape[1]
    past_len = page_table.shape[1] * cache_mla_ckv.shape[2]
    pad = BLOCK - n_meta
    lp = seq + BLOCK
    n_p = b * lp
    n_s = db * ds
    n = n_p + n_s
    alpha = (2 * depth) ** 0.25
    mla_scale = (qk_nope + qk_rope) ** -0.5
    swa_scale = swa_hd ** -0.5
    assert (qk_nope, qk_rope, v_head, swa_hd, win_buf) == (128, 64, 128, 64, WINDOW)
    assert n_p % 512 == 0 and n_s == 512 and n % 1024 == 0
    sample_block = n_p // n_s
    ds_pad = 16

    hp = jnp.concatenate([jnp.zeros((b, pad, d), x_prompt.dtype),
                          jnp.broadcast_to(meta_tokens.astype(x_prompt.dtype)[None], (b, n_meta, d)),
                          x_prompt], 1)
    x = jnp.concatenate([hp.reshape(n_p, d), x_sample.reshape(n_s, d)], 0)
    xb = x.astype(BF16)

    pos = jnp.concatenate([jnp.tile(jnp.arange(lp, dtype=jnp.int32) - pad, b),
                           jnp.tile(past_len + jnp.arange(ds, dtype=jnp.int32), db)])
    inv = ROPE_THETA ** (-jnp.arange(0, qk_rope, 2, dtype=F32) / qk_rope)
    ang = pos.astype(F32)[:, None] * inv
    cos, sin = jnp.cos(ang), jnp.sin(ang)
    ctab = jnp.pad(jnp.repeat(cos, 2, axis=-1), ((0, 0), (0, 128 - qk_rope)))
    stab = jnp.pad(jnp.stack([-sin, sin], -1).reshape(n, qk_rope), ((0, 0), (0, 128 - qk_rope)))

    pool_kpe_t = jnp.swapaxes(cache_mla_kpe, 2, 3)
    slopes = 2.0 ** (-ALIBI_MAX_BIAS * jnp.arange(1, swa_heads + 1, dtype=F32) / swa_heads)
    perm = _swa_slab_perm(swa_heads, swa_kv_heads, swa_hd)
    head_of_block = perm[::swa_hd] // swa_hd

    ckv_out, kpe_out = [], []
    kv = None
    for l in range(depth):
        g_mix, b_mix = ln_mix_g[l][None], ln_mix_b[l][None]
        if l < n_a:
            w_rope = w_dkv[l][:, kv_lora:]
            w_rope_sw = _pair_swap(w_rope)
            wr = jnp.concatenate([w_rope, w_rope_sw, w_rope_sw, w_rope], -1).astype(BF16)
            wq3 = w_uq[l].reshape(-1, heads, qk_nope + qk_rope)
            wq_rope = wq3[..., qk_nope:]
            wq_rope_sw = _pair_swap(wq_rope)
            wq2 = jnp.concatenate([wq3[..., :qk_nope], wq_rope, wq_rope_sw, wq_rope_sw, wq_rope], -1)
            wq2 = wq2.reshape(-1, heads * 384).astype(BF16)
            wk2 = w_uk[l].reshape(kv_lora, heads * qk_nope).astype(BF16)
            wv2 = w_uv[l].reshape(kv_lora, heads * v_head).astype(BF16)

            cq, ckv, ckvb, kpe = mla_down(xb, w_dq[l].astype(BF16), w_dkv[l][:, :kv_lora].astype(BF16), wr,
                                          g_q[l][None], g_kv[l][None], ctab, stab)
            q_full = mla_qup(cq, wq2, ctab, stab, heads)
            k_full, v_full = mla_kvup(ckvb, kpe, wk2, wv2, n_p, heads)
            attn_p = mla_prompt_attn(q_full, k_full, v_full, n_p, b, lp, heads, pad, mla_scale)

            q_lat = mla_absorb(q_full, wk2, sample_block, n_s, heads).reshape(n_s * heads, kv_lora)
            q_s = q_full[n_p:].reshape(n_s * heads, 256)
            ckv_new = jnp.pad(ckvb[n_p:].reshape(db, ds, kv_lora), ((0, 0), (0, ds_pad - ds), (0, 0)))
            kpe_new = jnp.pad(kpe[n_p:].reshape(db, ds, 128), ((0, 0), (0, ds_pad - ds), (0, 0)))
            o_lat = mla_decode(page_table, q_lat, q_s, ckv_new, kpe_new, cache_mla_ckv, pool_kpe_t,
                               l, heads, mla_scale)
            attn_s = mla_vup(o_lat.reshape(n_s, heads * kv_lora), wv2, heads)
            x, xb = proj_ln(attn_p, attn_s, w_o_mla[l].astype(BF16), x, g_mix, b_mix, alpha)
            ckv_out.append(ckv)
            kpe_out.append(kpe[:, :qk_rope])
        else:
            j = l - n_a
            if kv is None:
                w_kv = jnp.concatenate([w_k_shared, w_v_shared], -1).astype(BF16)
                kv = matmul_rows(xb, w_kv, F32)
                kvw = swa_kv_heads * swa_hd
                k_new = kv[n_p:, :kvw].reshape(db, ds, kvw)
                v_new = kv[n_p:, kvw:].reshape(db, ds, kvw)
                k_new_pad = jnp.pad(k_new, ((0, 0), (0, ds_pad - ds), (0, 0)))
                v_new_pad = jnp.pad(v_new, ((0, 0), (0, ds_pad - ds), (0, 0)))
            q = matmul_rows(xb, w_q_swa[j][:, perm].astype(BF16), BF16)
            hs = jnp.stack([slopes, sinks[j].astype(F32)])
            attn_p = swa_attn_prompt(hs, q, kv, n_p, b, lp, pad, swa_group, swa_scale)
            q_s = jnp.pad(q[n_p:].reshape(db, ds, -1), ((0, 0), (0, ds_pad - ds), (0, 0)))
            hrow = jnp.broadcast_to(jnp.repeat(hs[:, head_of_block], ds_pad, axis=1)[:, :, None],
                                    (2, head_of_block.size * ds_pad, 128))
            attn_s = swa_attn_sample(hrow, q_s, cache_swa_k.reshape(db, win_buf, kvw),
                                     cache_swa_v.reshape(db, win_buf, kvw), k_new_pad, v_new_pad,
                                     past_len, swa_group, swa_scale)
            x, xb = proj_ln(attn_p, attn_s[:, :ds].reshape(n_s, -1), w_o_swa[j][perm, :].astype(BF16),
                            x, g_mix, b_mix, alpha)

        g_ffn, b_ffn = ln_ffn_g[l][None], ln_ffn_b[l][None]
        f = l // 2
        if l % 2 == 0:
            h = ffn_gate_up(xb, w_gate, w_up, f)
            x, xb = ffn_down_ln(h, w_down, f, x, g_ffn, b_ffn, alpha)
        else:
            x, xb = moe_ffn_ln(x, w_router, we_gate, we_up, we_down, f, g_ffn, b_ffn, alpha)

    hp_out = x[:n_p].reshape(b, lp, d)
    y_prompt = hp_out[:, BLOCK:]
    y_sample = x[n_p:].reshape(db, ds, d)
    new_ckv_prompt = jnp.stack([c[:n_p].reshape(b, lp, kv_lora)[:, pad:] for c in ckv_out])
    new_kpe_prompt = jnp.stack([c[:n_p].reshape(b, lp, qk_rope)[:, pad:] for c in kpe_out])
    new_ckv_sample = jnp.stack([c[n_p:].reshape(db, ds, kv_lora) for c in ckv_out])
    new_kpe_sample = jnp.stack([c[n_p:].reshape(db, ds, qk_rope) for c in kpe_out])
    wp = min(WINDOW, seq + n_meta)
    k_p = kv[:n_p, :kvw].reshape(b, lp, swa_kv_heads, swa_hd)
    v_p = kv[:n_p, kvw:].reshape(b, lp, swa_kv_heads, swa_hd)
    new_swa_k_prompt = k_p[:, lp - wp:]
    new_swa_v_prompt = v_p[:, lp - wp:]
    new_swa_k_sample = jnp.concatenate([cache_swa_k, k_new.reshape(db, ds, swa_kv_heads, swa_hd)], 1)[:, ds:]
    new_swa_v_sample = jnp.concatenate([cache_swa_v, v_new.reshape(db, ds, swa_kv_heads, swa_hd)], 1)[:, ds:]
    return (y_prompt, y_sample, new_ckv_prompt, new_kpe_prompt, new_ckv_sample, new_kpe_sample,
            new_swa_k_prompt, new_swa_v_prompt, new_swa_k_sample, new_swa_v_sample)
```

```python
import functools

import numpy as np
import jax
import jax.numpy as jnp
from jax import lax
from jax.experimental import pallas as pl
from jax.experimental.pallas import tpu as pltpu

F32 = jnp.float32
BF16 = jnp.bfloat16

BLOCK = 128
WINDOW = 128
ROPE_THETA = 10000.0
ALIBI_MAX_BIAS = 8.0
LN_EPS = 1e-5
RMS_EPS = 1e-6
NEG_INF = -1e30
LOG2E = 1.4426950408889634
N_EXPERT_LANES = 128
ROW_DMA_UNROLL = 8
MOE_TILES_PER_GROUP = 5

V7X_VMEM_BYTES = 64 * 1024 * 1024
VMEM_CEILING = V7X_VMEM_BYTES - 6 * 1024 * 1024
MIB = 1024 * 1024


def _params(semantics, vmem_mib):
    return pltpu.CompilerParams(dimension_semantics=semantics,
                                vmem_limit_bytes=min(int(vmem_mib * MIB), VMEM_CEILING))


def _nt(a, b):
    return lax.dot_general(a, b, (((1,), (1,)), ((), ())), preferred_element_type=F32)


def _nn(a, b):
    return jnp.dot(a, b, preferred_element_type=F32)


def _rms(x, g):
    return x * lax.rsqrt(jnp.mean(jnp.square(x), -1, keepdims=True) + RMS_EPS) * g


def _ln_store(z, g_ref, b_ref, o_ref, ob_ref):
    mu = jnp.mean(z, -1, keepdims=True)
    d = z - mu
    var = jnp.mean(jnp.square(d), -1, keepdims=True)
    y = d * lax.rsqrt(var + LN_EPS) * g_ref[...] + b_ref[...]
    o_ref[...] = y
    ob_ref[...] = y.astype(ob_ref.dtype)


def _silu(g):
    return g * (1.0 / (1.0 + jnp.exp(-g)))


def _mm_kernel(x_ref, w_ref, o_ref):
    o_ref[...] = _nn(x_ref[...], w_ref[...]).astype(o_ref.dtype)


def matmul_rows(x, w, out_dtype, tm=512):
    m, k = x.shape
    n = w.shape[1]
    return pl.pallas_call(
        _mm_kernel,
        grid=(m // tm,),
        in_specs=[pl.BlockSpec((tm, k), lambda i: (i, 0)),
                  pl.BlockSpec((k, n), lambda i: (0, 0))],
        out_specs=pl.BlockSpec((tm, n), lambda i: (i, 0)),
        out_shape=jax.ShapeDtypeStruct((m, n), out_dtype),
        compiler_params=_params(("parallel",), 40),
        name="matmul_rows",
    )(x, w)


def _mla_down_kernel(x_ref, wq_ref, wkv_ref, wr_ref, gq_ref, gkv_ref, c_ref, s_ref,
                     cq_ref, ckv_ref, ckvb_ref, kpe_ref):
    x = x_ref[...]
    cq_ref[...] = _rms(_nn(x, wq_ref[...]), gq_ref[...]).astype(cq_ref.dtype)
    ckv = _rms(_nn(x, wkv_ref[...]), gkv_ref[...])
    ckv_ref[...] = ckv
    ckvb_ref[...] = ckv.astype(ckvb_ref.dtype)
    r = _nn(x, wr_ref[...])
    half = r.shape[1] // 2
    kpe_ref[...] = r[:, :half] * c_ref[...] + r[:, half:] * s_ref[...]


def mla_down(xb, wq, wkv, wr, gq, gkv, ctab, stab, tm=512):
    n, d = xb.shape
    ql, kl = wq.shape[1], wkv.shape[1]
    row = lambda w: pl.BlockSpec((tm, w), lambda i: (i, 0))
    full = lambda a: pl.BlockSpec(a.shape, lambda i: (0,) * a.ndim)
    return pl.pallas_call(
        _mla_down_kernel,
        grid=(n // tm,),
        in_specs=[row(d), full(wq), full(wkv), full(wr), full(gq), full(gkv), row(128), row(128)],
        out_specs=[row(ql), row(kl), row(kl), row(128)],
        out_shape=[jax.ShapeDtypeStruct((n, ql), BF16), jax.ShapeDtypeStruct((n, kl), F32),
                   jax.ShapeDtypeStruct((n, kl), BF16), jax.ShapeDtypeStruct((n, 128), F32)],
        compiler_params=_params(("parallel",), 40),
        name="mla_down",
    )(xb, wq, wkv, wr, gq, gkv, ctab, stab)


def _mla_qup_kernel(cq_ref, w_ref, c_ref, s_ref, o_ref, *, heads_per_step):
    cq = cq_ref[...]
    c = c_ref[...]
    s = s_ref[...]
    for h in range(heads_per_step):
        acc = _nn(cq, w_ref[:, h * 384:(h + 1) * 384])
        o_ref[:, h * 256:h * 256 + 128] = acc[:, :128].astype(o_ref.dtype)
        o_ref[:, h * 256 + 128:(h + 1) * 256] = (acc[:, 128:256] * c + acc[:, 256:384] * s).astype(o_ref.dtype)


def mla_qup(cq, w, ctab, stab, heads, tm=512, hps=4):
    n, ql = cq.shape
    return pl.pallas_call(
        functools.partial(_mla_qup_kernel, heads_per_step=hps),
        grid=(n // tm, heads // hps),
        in_specs=[pl.BlockSpec((tm, ql), lambda i, j: (i, 0)),
                  pl.BlockSpec((ql, hps * 384), lambda i, j: (0, j)),
                  pl.BlockSpec((tm, 128), lambda i, j: (i, 0)),
                  pl.BlockSpec((tm, 128), lambda i, j: (i, 0))],
        out_specs=pl.BlockSpec((tm, hps * 256), lambda i, j: (i, j)),
        out_shape=jax.ShapeDtypeStruct((n, heads * 256), BF16),
        compiler_params=_params(("parallel", "parallel"), 32),
        name="mla_qup",
    )(cq, w, ctab, stab)


def _mla_kvup_kernel(c_ref, kpe_ref, wk_ref, wv_ref, k_ref, v_ref, *, heads_per_step):
    c = c_ref[...]
    kn = _nn(c, wk_ref[...])
    pe = kpe_ref[...].astype(k_ref.dtype)
    for h in range(heads_per_step):
        k_ref[:, h * 256:h * 256 + 128] = kn[:, h * 128:(h + 1) * 128].astype(k_ref.dtype)
        k_ref[:, h * 256 + 128:(h + 1) * 256] = pe
    v_ref[...] = _nn(c, wv_ref[...]).astype(v_ref.dtype)


def mla_kvup(ckvb, kpe, wk, wv, rows, heads, tm=512, hps=4):
    kl = ckvb.shape[1]
    return pl.pallas_call(
        functools.partial(_mla_kvup_kernel, heads_per_step=hps),
        grid=(rows // tm, heads // hps),
        in_specs=[pl.BlockSpec((tm, kl), lambda i, j: (i, 0)),
                  pl.BlockSpec((tm, 128), lambda i, j: (i, 0)),
                  pl.BlockSpec((kl, hps * 128), lambda i, j: (0, j)),
                  pl.BlockSpec((kl, hps * 128), lambda i, j: (0, j))],
        out_specs=[pl.BlockSpec((tm, hps * 256), lambda i, j: (i, j)),
                   pl.BlockSpec((tm, hps * 128), lambda i, j: (i, j))],
        out_shape=[jax.ShapeDtypeStruct((rows, heads * 256), BF16),
                   jax.ShapeDtypeStruct((rows, heads * 128), BF16)],
        compiler_params=_params(("parallel", "parallel"), 32),
        name="mla_kvup",
    )(ckvb, kpe, wk, wv)


def _mla_prompt_attn_kernel(q_ref, k_ref, v_ref, o_ref, *, blocks, pad, scale):
    scale2 = scale * LOG2E
    for r0, r1 in blocks:
        q = q_ref[r0:r1, :]
        pieces = [(0, min(BLOCK, r0)), (BLOCK, r0), (r0, r1)]
        scores = []
        for c0, c1 in pieces:
            if c1 <= c0:
                continue
            s = _nt(q, k_ref[c0:c1, :]) * scale2
            if c0 < pad:
                col = lax.broadcasted_iota(jnp.int32, s.shape, 1) + c0
                s = jnp.where(col >= pad, s, NEG_INF)
            if c1 > r0:
                row = lax.broadcasted_iota(jnp.int32, s.shape, 0) + r0
                col = lax.broadcasted_iota(jnp.int32, s.shape, 1) + c0
                s = jnp.where(col <= row, s, NEG_INF)
            scores.append((c0, c1, s))
        m = functools.reduce(jnp.maximum, [jnp.max(s, -1, keepdims=True) for _, _, s in scores])
        l = 0.0
        o = 0.0
        for c0, c1, s in scores:
            p = jnp.exp2(s - m)
            l = l + jnp.sum(p, -1, keepdims=True)
            o = o + _nn(p.astype(v_ref.dtype), v_ref[c0:c1, :])
        o_ref[r0:r1, :] = (o / l).astype(o_ref.dtype)


def mla_prompt_attn(q, k, v, n_rows, batch, lp, heads, pad, scale):
    blocks = [(0, BLOCK)] + [(r, r + 256) for r in range(BLOCK, lp, 256)]
    assert blocks[-1][1] == lp
    return pl.pallas_call(
        functools.partial(_mla_prompt_attn_kernel, blocks=tuple(blocks), pad=pad, scale=scale),
        grid=(batch, heads),
        in_specs=[pl.BlockSpec((lp, 256), lambda b, h: (b, h)),
                  pl.BlockSpec((lp, 256), lambda b, h: (b, h)),
                  pl.BlockSpec((lp, 128), lambda b, h: (b, h))],
        out_specs=pl.BlockSpec((lp, 128), lambda b, h: (b, h)),
        out_shape=jax.ShapeDtypeStruct((n_rows, heads * 128), BF16),
        compiler_params=_params(("parallel", "parallel"), 40),
        name="mla_prompt_attn",
    )(q, k, v)


def _mla_absorb_kernel(q_ref, w_ref, o_ref):
    o_ref[...] = _nt(q_ref[:, :128], w_ref[...]).astype(o_ref.dtype)


def mla_absorb(q_full, wk, row_block, rows, heads):
    kl = wk.shape[0]
    return pl.pallas_call(
        _mla_absorb_kernel,
        grid=(heads,),
        in_specs=[pl.BlockSpec((rows, 256), lambda h: (row_block, h)),
                  pl.BlockSpec((kl, 128), lambda h: (0, h))],
        out_specs=pl.BlockSpec((rows, kl), lambda h: (0, h)),
        out_shape=jax.ShapeDtypeStruct((rows, heads * kl), BF16),
        compiler_params=_params(("parallel",), 32),
        name="mla_absorb",
    )(q_full, wk)


def _mla_decode_kernel(pt_ref, ql_ref, q_ref, cn_ref, kn_ref, ckv_hbm, kpe_hbm, o_ref,
                       ckv_buf, kpe_buf, sem, m_sc, l_sc, acc_sc, *, layer, pages, scale, heads, rope):
    j = pl.program_id(1)
    n_chunks = pl.num_programs(1)
    step = pl.program_id(0) * n_chunks + j
    page = ckv_hbm.shape[2]

    def page_copies(page_id, slot, i):
        return (pltpu.make_async_copy(ckv_hbm.at[layer, page_id], ckv_buf.at[slot, pl.ds(i * page, page), :],
                                      sem.at[0, slot]),
                pltpu.make_async_copy(kpe_hbm.at[layer, page_id], kpe_buf.at[slot, i], sem.at[1, slot]))

    def start_chunk(chunk, slot):
        for i in range(pages):
            for copy in page_copies(pt_ref[chunk * pages + i], slot, i):
                copy.start()

    @pl.when(step == 0)
    def _():
        start_chunk(0, 0)

    @pl.when(step + 1 < pl.num_programs(0) * n_chunks)
    def _():
        start_chunk(step + 1, (step + 1) % 2)

    slot = step % 2
    for i in range(pages):
        for copy in page_copies(0, slot, i):
            copy.wait()

    @pl.when(j == 0)
    def _():
        m_sc[...] = jnp.full(m_sc.shape, NEG_INF, F32)
        l_sc[...] = jnp.zeros(l_sc.shape, F32)
        acc_sc[...] = jnp.zeros(acc_sc.shape, F32)

    ql = ql_ref[...]
    qp = q_ref[:, 128:128 + rope]
    scale2 = scale * LOG2E

    def update(s, vals):
        m_prev = m_sc[...]
        m_new = jnp.maximum(m_prev, jnp.max(s, -1, keepdims=True))
        a = jnp.exp2(m_prev - m_new)
        p = jnp.exp2(s - m_new)
        l_sc[...] = a * l_sc[...] + jnp.sum(p, -1, keepdims=True)
        acc_sc[...] = a * acc_sc[...] + _nn(p.astype(vals.dtype), vals)
        m_sc[...] = m_new

    ck = ckv_buf[slot].astype(BF16)
    kpt = jnp.concatenate([kpe_buf[slot, i].astype(BF16) for i in range(pages)], axis=1)
    update((_nt(ql, ck) + _nn(qp, kpt)) * scale2, ck)

    @pl.when(j == pl.num_programs(1) - 1)
    def _():
        cn = cn_ref[...]
        kn = kn_ref[:, :rope].astype(BF16)
        s = (_nt(ql, cn) + _nt(qp, kn)) * scale2
        tok = lax.broadcasted_iota(jnp.int32, s.shape, 0) // heads
        col = lax.broadcasted_iota(jnp.int32, s.shape, 1)
        update(jnp.where(col <= tok, s, NEG_INF), cn)
        o_ref[...] = (acc_sc[...] / l_sc[...]).astype(o_ref.dtype)


def mla_decode(page_table, q_lat, q_s, ckv_new, kpe_new, pool_ckv, pool_kpe_t, layer, heads, scale, pages=32):
    db, n_pages = page_table.shape
    page, kl = pool_ckv.shape[2], pool_ckv.shape[3]
    rope = pool_kpe_t.shape[2]
    rows = q_lat.shape[0] // db
    new_rows = ckv_new.shape[1]
    pages = min(pages, n_pages)
    assert n_pages % pages == 0
    pt = page_table.reshape(-1)
    hbm = pl.BlockSpec(memory_space=pl.ANY)
    in_specs = [pl.BlockSpec((rows, kl), lambda b, j, pt_ref: (b, 0)),
                pl.BlockSpec((rows, 256), lambda b, j, pt_ref: (b, 0)),
                pl.BlockSpec((None, new_rows, kl), lambda b, j, pt_ref: (b, 0, 0)),
                pl.BlockSpec((None, new_rows, 128), lambda b, j, pt_ref: (b, 0, 0)),
                hbm, hbm]
    return pl.pallas_call(
        functools.partial(_mla_decode_kernel, layer=layer, pages=pages, scale=scale, heads=heads, rope=rope),
        grid_spec=pltpu.PrefetchScalarGridSpec(
            num_scalar_prefetch=1,
            grid=(db, n_pages // pages),
            in_specs=in_specs,
            out_specs=pl.BlockSpec((rows, kl), lambda b, j, pt_ref: (b, 0)),
            scratch_shapes=[pltpu.VMEM((2, pages * page, kl), F32), pltpu.VMEM((2, pages, rope, page), F32),
                            pltpu.SemaphoreType.DMA((2, 2)),
                            pltpu.VMEM((rows, 1), F32), pltpu.VMEM((rows, 1), F32),
                            pltpu.VMEM((rows, kl), F32)]),
        out_shape=jax.ShapeDtypeStruct(q_lat.shape, BF16),
        compiler_params=_params(("arbitrary", "arbitrary"), 48),
        name="mla_decode",
    )(pt, q_lat, q_s, ckv_new, kpe_new, pool_ckv, pool_kpe_t)


def mla_vup(o_lat2d, wv, heads):
    rows = o_lat2d.shape[0]
    kl = wv.shape[0]
    return pl.pallas_call(
        _mm_kernel,
        grid=(heads,),
        in_specs=[pl.BlockSpec((rows, kl), lambda h: (0, h)),
                  pl.BlockSpec((kl, 128), lambda h: (0, h))],
        out_specs=pl.BlockSpec((rows, 128), lambda h: (0, h)),
        out_shape=jax.ShapeDtypeStruct((rows, heads * 128), BF16),
        compiler_params=_params(("parallel",), 32),
        name="mla_vup",
    )(o_lat2d, wv)


def _proj_ln_kernel(xp_ref, xs_ref, w_ref, r_ref, g_ref, b_ref, o_ref, ob_ref, *, alpha, prompt_blocks):
    x = jnp.where(pl.program_id(0) < prompt_blocks, xp_ref[...], xs_ref[...])
    z = alpha * r_ref[...] + _nn(x, w_ref[...])
    _ln_store(z, g_ref, b_ref, o_ref, ob_ref)


def proj_ln(x_p, x_s, w, resid, g, b, alpha, tm=512):
    k = x_p.shape[1]
    n, d = resid.shape
    pb = x_p.shape[0] // tm
    row = lambda wd: pl.BlockSpec((tm, wd), lambda i: (i, 0))
    full = lambda a: pl.BlockSpec(a.shape, lambda i: (0,) * a.ndim)
    return pl.pallas_call(
        functools.partial(_proj_ln_kernel, alpha=alpha, prompt_blocks=pb),
        grid=(n // tm,),
        in_specs=[pl.BlockSpec((tm, k), lambda i: (jnp.minimum(i, pb - 1), 0)),
                  pl.BlockSpec((tm, k), lambda i: (jnp.maximum(i - pb, 0), 0)),
                  full(w), row(d), full(g), full(b)],
        out_specs=[row(d), row(d)],
        out_shape=[jax.ShapeDtypeStruct((n, d), F32), jax.ShapeDtypeStruct((n, d), BF16)],
        compiler_params=_params(("parallel",), 48),
        name="proj_ln",
    )(x_p, x_s, w, resid, g, b)


def _ffn_gu_kernel(x_ref, wg_ref, wu_ref, o_ref, wgb, wub):
    @pl.when(pl.program_id(1) == 0)
    def _():
        wgb[...] = wg_ref[...].astype(BF16)
        wub[...] = wu_ref[...].astype(BF16)

    x = x_ref[...]
    g = _nn(x, wgb[...])
    u = _nn(x, wub[...])
    o_ref[...] = (_silu(g) * u).astype(o_ref.dtype)


def ffn_gate_up(xb, w_gate, w_up, f, tm=1024, tn=512):
    n, d = xb.shape
    dff = w_gate.shape[2]
    wspec = pl.BlockSpec((None, d, tn), lambda j, i: (f, 0, j))
    return pl.pallas_call(
        _ffn_gu_kernel,
        grid=(dff // tn, n // tm),
        in_specs=[pl.BlockSpec((tm, d), lambda j, i: (i, 0)), wspec, wspec],
        out_specs=pl.BlockSpec((tm, tn), lambda j, i: (i, j)),
        out_shape=jax.ShapeDtypeStruct((n, dff), BF16),
        scratch_shapes=[pltpu.VMEM((d, tn), BF16), pltpu.VMEM((d, tn), BF16)],
        compiler_params=_params(("arbitrary", "arbitrary"), 48),
        name="ffn_gate_up",
    )(xb, w_gate, w_up)


def _ffn_down_ln_kernel(h_ref, w_ref, r_ref, g_ref, b_ref, o_ref, ob_ref, acc, *, alpha):
    k = pl.program_id(1)

    @pl.when(k == 0)
    def _():
        acc[...] = jnp.zeros(acc.shape, F32)

    acc[...] += _nn(h_ref[...], w_ref[...].astype(BF16))

    @pl.when(k == pl.num_programs(1) - 1)
    def _():
        _ln_store(alpha * r_ref[...] + acc[...], g_ref, b_ref, o_ref, ob_ref)


def ffn_down_ln(h, w_down, f, resid, g, b, alpha, tm=1024, tk=512):
    n, dff = h.shape
    d = w_down.shape[2]
    row = pl.BlockSpec((tm, d), lambda i, k: (i, 0), pipeline_mode=pl.Buffered(1))
    vec = lambda a: pl.BlockSpec(a.shape, lambda i, k: (0,) * a.ndim)
    return pl.pallas_call(
        functools.partial(_ffn_down_ln_kernel, alpha=alpha),
        grid=(n // tm, dff // tk),
        in_specs=[pl.BlockSpec((tm, tk), lambda i, k: (i, k)),
                  pl.BlockSpec((None, tk, d), lambda i, k: (f, k, 0)),
                  row, vec(g), vec(b)],
        out_specs=[row, row],
        out_shape=[jax.ShapeDtypeStruct((n, d), F32), jax.ShapeDtypeStruct((n, d), BF16)],
        scratch_shapes=[pltpu.VMEM((tm, d), F32)],
        compiler_params=_params(("parallel", "arbitrary"), 56),
        name="ffn_down_ln",
    )(h, w_down, resid, g, b)


def _router_kernel(x_ref, w_ref, info_ref, cnt_ref, x3_ref, carry, *, n_experts):
    i = pl.program_id(0)

    @pl.when(i == 0)
    def _():
        carry[...] = jnp.zeros(carry.shape, F32)

    logits = jnp.dot(x_ref[...], w_ref[...], preferred_element_type=F32, precision=lax.Precision.HIGHEST)
    tm = logits.shape[0]
    lane = lax.broadcasted_iota(jnp.int32, logits.shape, 1).astype(F32)
    lowest = float(np.finfo(np.float32).min)
    lg = jnp.where(lane < n_experts, logits, lowest)
    v1 = jnp.max(lg, -1, keepdims=True)
    i1 = jnp.min(jnp.where(lg == v1, lane, float(N_EXPERT_LANES)), -1, keepdims=True)
    lg2 = jnp.where(lane == i1, lowest, lg)
    v2 = jnp.max(lg2, -1, keepdims=True)
    i2 = jnp.min(jnp.where(lg2 == v2, lane, float(N_EXPERT_LANES)), -1, keepdims=True)
    e2 = jnp.exp(v2 - v1)
    w1 = 1.0 / (1.0 + e2)
    w2 = e2 / (1.0 + e2)
    sel1 = lane == i1
    sel2 = lane == i2
    onehot = jnp.where(sel1, 1.0, jnp.where(sel2, 1.0, 0.0))
    r = lax.broadcasted_iota(jnp.int32, (tm, tm), 0)
    c = lax.broadcasted_iota(jnp.int32, (tm, tm), 1)
    lower = jnp.where(c < r, 1.0, 0.0).astype(BF16)
    before = _nn(lower, onehot.astype(BF16)) + carry[...]
    rank1 = jnp.sum(jnp.where(sel1, before, 0.0), -1, keepdims=True)
    rank2 = jnp.sum(jnp.where(sel2, before, 0.0), -1, keepdims=True)
    total = carry[...] + jnp.sum(onehot, 0, keepdims=True)
    carry[...] = total
    cols = (i1, i2, rank1, rank2, w1, w2)
    info = jnp.zeros(logits.shape, F32)
    for k, val in enumerate(cols):
        info = jnp.where(lane == k, val, info)
    info_ref[...] = info
    cnt_ref[...] = jnp.broadcast_to(total, cnt_ref.shape)
    chunks = x_ref.shape[1] // 128
    for j in range(chunks):
        x3_ref[pl.ds(j, tm, stride=chunks), :] = x_ref[:, j * 128:(j + 1) * 128]


def moe_router(x, w_pad, n_experts, tm=512):
    n, d = x.shape
    chunks = d // 128
    return pl.pallas_call(
        functools.partial(_router_kernel, n_experts=n_experts),
        grid=(n // tm,),
        in_specs=[pl.BlockSpec((tm, d), lambda i: (i, 0)),
                  pl.BlockSpec(w_pad.shape, lambda i: (0, 0))],
        out_specs=[pl.BlockSpec((tm, N_EXPERT_LANES), lambda i: (i, 0)),
                   pl.BlockSpec((8, N_EXPERT_LANES), lambda i: (0, 0)),
                   pl.BlockSpec((tm * chunks, 128), lambda i: (i, 0))],
        out_shape=[jax.ShapeDtypeStruct((n, N_EXPERT_LANES), F32),
                   jax.ShapeDtypeStruct((8, N_EXPERT_LANES), F32),
                   jax.ShapeDtypeStruct((n * chunks, 128), F32)],
        scratch_shapes=[pltpu.VMEM((1, N_EXPERT_LANES), F32)],
        compiler_params=_params(("arbitrary",), 40),
        name="moe_router",
    )(x, w_pad)


def _moe_gather_kernel(tok_ref, x3_hbm, o_ref, buf, sem):
    tm = o_ref.shape[0]
    chunks = x3_hbm.shape[1]
    i = pl.program_id(0)

    def row_copy(token, slot, r):
        return pltpu.make_async_copy(x3_hbm.at[token], buf.at[slot, pl.ds(r * chunks, chunks), :], sem.at[slot])

    def start_tile(tile, slot):
        def body(g, carry):
            for u in range(ROW_DMA_UNROLL):
                r = g * ROW_DMA_UNROLL + u
                row_copy(tok_ref[tile * tm + r], slot, r).start(priority=u % 2)
            return carry
        lax.fori_loop(0, tm // ROW_DMA_UNROLL, body, 0)

    @pl.when(i == 0)
    def _():
        start_tile(0, 0)

    @pl.when(i + 1 < pl.num_programs(0))
    def _():
        start_tile(i + 1, (i + 1) % 2)

    slot = i % 2

    def wait(r, carry):
        row_copy(0, slot, r).wait()
        return carry

    lax.fori_loop(0, tm, wait, 0, unroll=8)
    for j in range(chunks):
        o_ref[:, j * 128:(j + 1) * 128] = buf[slot, pl.ds(j, tm, stride=chunks), :].astype(o_ref.dtype)


def moe_gather(token_of_row, x3, tm=256):
    rows = token_of_row.shape[0]
    chunks = x3.shape[1]
    return pl.pallas_call(
        _moe_gather_kernel,
        grid_spec=pltpu.PrefetchScalarGridSpec(
            num_scalar_prefetch=1,
            grid=(rows // tm,),
            in_specs=[pl.BlockSpec(memory_space=pl.ANY)],
            out_specs=pl.BlockSpec((tm, chunks * 128), lambda i, tok: (i, 0)),
            scratch_shapes=[pltpu.VMEM((2, tm * chunks, 128), F32), pltpu.SemaphoreType.DMA((2,))]),
        out_shape=jax.ShapeDtypeStruct((rows, chunks * 128), BF16),
        compiler_params=_params(("arbitrary",), 32),
        name="moe_gather",
    )(token_of_row, x3)


def _moe_experts_kernel(se_ref, st_ref, sn_ref, sz_ref, xs_hbm, wg_ref, wu_ref, wd_ref, y_hbm,
                        x_buf, acc, sem, *, tile, tiles_per_group):
    s = pl.program_id(0)
    f = pl.program_id(1)
    n_compute = sn_ref[s]
    n_zero = sz_ref[s]
    first_tile = st_ref[s]

    def rows(t):
        return pl.ds(t * tile, tile)

    def tile_in(t):
        return pltpu.make_async_copy(xs_hbm.at[pl.ds((first_tile + t) * tile, tile), :], x_buf.at[rows(t), :],
                                     sem.at[0])

    def tile_out(t):
        return pltpu.make_async_copy(acc.at[rows(t), :], y_hbm.at[pl.ds((first_tile + t) * tile, tile), :],
                                     sem.at[1])

    def for_tiles(count, fn):
        for t in range(tiles_per_group):
            @pl.when(t < count)
            def _(t=t):
                fn(t)

    @pl.when(f == 0)
    def _():
        for_tiles(n_compute, lambda t: tile_in(t).start())
        acc[...] = jnp.zeros(acc.shape, F32)
        for_tiles(n_compute, lambda t: tile_in(t).wait())
        for_tiles(n_zero, lambda t: tile_out(t).start())
        for_tiles(n_zero, lambda t: tile_out(t).wait())

    @pl.when(n_compute > 0)
    def _():
        tf = wg_ref.shape[1]
        w_gu = jnp.concatenate([wg_ref[...].astype(BF16), wu_ref[...].astype(BF16)], axis=1)
        w_d = wd_ref[...].astype(BF16)

        def compute(t):
            gu = _nn(x_buf[rows(t), :], w_gu)
            h = (_silu(gu[:, :tf]) * gu[:, tf:]).astype(BF16)
            acc[rows(t), :] += _nn(h, w_d)

        for_tiles(n_compute, compute)

    @pl.when(f == pl.num_programs(1) - 1)
    def _():
        for_tiles(n_compute, lambda t: tile_out(t).start())
        for_tiles(n_compute, lambda t: tile_out(t).wait())


def moe_experts(sup_expert, sup_tile, sup_compute, sup_zero, xs, we_gate, we_up, we_down, layer,
                tile, tiles_per_group, tf=256):
    rows, d = xs.shape
    dff = we_gate.shape[3]
    n_groups = sup_expert.shape[0]
    n_f = dff // tf

    def f_block(s, f, sn):
        return jnp.where(sn[s] > 0, f, n_f - 1)

    gu_spec = pl.BlockSpec((None, None, d, tf), lambda s, f, se, st, sn, sz: (layer, se[s], 0, f_block(s, f, sn)))
    d_spec = pl.BlockSpec((None, None, tf, d), lambda s, f, se, st, sn, sz: (layer, se[s], f_block(s, f, sn), 0))
    hbm = pl.BlockSpec(memory_space=pl.ANY)
    return pl.pallas_call(
        functools.partial(_moe_experts_kernel, tile=tile, tiles_per_group=tiles_per_group),
        grid_spec=pltpu.PrefetchScalarGridSpec(
            num_scalar_prefetch=4,
            grid=(n_groups, n_f),
            in_specs=[hbm, gu_spec, gu_spec, d_spec],
            out_specs=hbm,
            scratch_shapes=[pltpu.VMEM((tiles_per_group * tile, d), BF16),
                            pltpu.VMEM((tiles_per_group * tile, d), F32),
                            pltpu.SemaphoreType.DMA((2,))]),
        out_shape=jax.ShapeDtypeStruct((rows, d), F32),
        compiler_params=_params(("arbitrary", "arbitrary"), 58),
        name="moe_experts",
    )(sup_expert, sup_tile, sup_compute, sup_zero, xs, we_gate, we_up, we_down)


def _moe_combine_ln_kernel(p1_ref, p2_ref, y_hbm, info_ref, r_ref, g_ref, b_ref, o_ref, ob_ref,
                           buf, sem, *, alpha):
    tm = o_ref.shape[0]
    i = pl.program_id(0)
    pos_refs = (p1_ref, p2_ref)

    def row_copy(src_row, slot, which, r):
        return pltpu.make_async_copy(y_hbm.at[pl.ds(src_row, 1), :], buf.at[slot, which, pl.ds(r, 1), :],
                                     sem.at[slot, which])

    def start_tile(tile, slot):
        def body(g, carry):
            for u in range(ROW_DMA_UNROLL):
                r = g * ROW_DMA_UNROLL + u
                for which in range(2):
                    row_copy(pos_refs[which][tile * tm + r], slot, which, r).start(priority=which)
            return carry
        lax.fori_loop(0, tm // ROW_DMA_UNROLL, body, 0)

    @pl.when(i == 0)
    def _():
        start_tile(0, 0)

    @pl.when(i + 1 < pl.num_programs(0))
    def _():
        start_tile(i + 1, (i + 1) % 2)

    slot = i % 2

    def wait(r, carry):
        for which in range(2):
            row_copy(0, slot, which, r).wait()
        return carry

    lax.fori_loop(0, tm, wait, 0, unroll=ROW_DMA_UNROLL)
    info = info_ref[...]
    f = info[:, 4:5] * buf[slot, 0] + info[:, 5:6] * buf[slot, 1]
    _ln_store(alpha * r_ref[...] + f, g_ref, b_ref, o_ref, ob_ref)


def moe_combine_ln(pos1, pos2, y, info, resid, g, b, alpha, tm=256):
    n, d = resid.shape
    row = lambda w: pl.BlockSpec((tm, w), lambda i, p1, p2: (i, 0))
    vec = lambda a: pl.BlockSpec(a.shape, lambda i, p1, p2: (0,) * a.ndim)
    return pl.pallas_call(
        functools.partial(_moe_combine_ln_kernel, alpha=alpha),
        grid_spec=pltpu.PrefetchScalarGridSpec(
            num_scalar_prefetch=2,
            grid=(n // tm,),
            in_specs=[pl.BlockSpec(memory_space=pl.ANY), row(N_EXPERT_LANES), row(d), vec(g), vec(b)],
            out_specs=[row(d), row(d)],
            scratch_shapes=[pltpu.VMEM((2, 2, tm, d), F32), pltpu.SemaphoreType.DMA((2, 2))]),
        out_shape=[jax.ShapeDtypeStruct((n, d), F32), jax.ShapeDtypeStruct((n, d), BF16)],
        compiler_params=_params(("arbitrary",), 40),
        name="moe_combine_ln",
    )(pos1, pos2, y, info, resid, g, b)


def moe_ffn_ln(x, w_router, we_gate, we_up, we_down, f, g, b, alpha, tile=512):
    n, d = x.shape
    n_experts = w_router.shape[2]
    w_pad = jnp.pad(w_router[f], ((0, 0), (0, N_EXPERT_LANES - n_experts)))
    info, cnt, x3 = moe_router(x, w_pad, n_experts)
    e1 = info[:, 0].astype(jnp.int32)
    e2 = info[:, 1].astype(jnp.int32)
    counts = cnt[0, :n_experts].astype(jnp.int32)
    group = (counts + tile - 1) // tile * tile
    ends = jnp.cumsum(group)
    starts = ends - group
    pos1 = starts[e1] + info[:, 2].astype(jnp.int32)
    pos2 = starts[e2] + info[:, 3].astype(jnp.int32)
    rows = 2 * n + n_experts * tile
    tok = jnp.arange(n, dtype=jnp.int32)
    token_of_row = jnp.zeros((rows,), jnp.int32).at[jnp.concatenate([pos1, pos2])].set(
        jnp.concatenate([tok, tok]), unique_indices=True)
    tpg = MOE_TILES_PER_GROUP
    n_tiles = rows // tile
    n_valid = ends[-1] // tile
    tiles_e = group // tile
    groups_e = (tiles_e + tpg - 1) // tpg
    g_end = jnp.cumsum(groups_e)
    g_start = g_end - groups_e
    n_compute = g_end[-1]
    n_groups = (n_tiles + (tpg - 1) * (n_experts + 1)) // tpg + 1
    sidx = jnp.arange(n_groups, dtype=jnp.int32)
    e_of = jnp.minimum(jnp.sum((sidx[:, None] >= g_end[None, :]).astype(jnp.int32), -1), n_experts - 1)
    k = sidx - g_start[e_of]
    is_compute = sidx < n_compute
    sup_compute = jnp.where(is_compute, jnp.clip(tiles_e[e_of] - k * tpg, 0, tpg), 0)
    zero_tile = n_valid + (sidx - n_compute) * tpg
    sup_zero = jnp.where(is_compute, 0, jnp.clip(n_tiles - zero_tile, 0, tpg))
    sup_tile = jnp.where(is_compute, starts[e_of] // tile + k * tpg, jnp.where(sup_zero > 0, zero_tile, 0))
    sup_expert = jnp.where(is_compute, e_of, e_of[n_compute - 1])
    as_i32 = lambda a: a.astype(jnp.int32)

    xs = moe_gather(token_of_row, x3.reshape(n, d // 128, 128))
    y = moe_experts(as_i32(sup_expert), as_i32(sup_tile), as_i32(sup_compute), as_i32(sup_zero), xs,
                    we_gate, we_up, we_down, f, tile, tpg)
    return moe_combine_ln(pos1, pos2, y, info, x, g, b, alpha)


def _swa_kernel(hs_ref, q_ref, kp_ref, kc_ref, vp_ref, vc_ref, o_ref, *, tq, group, scale, start_fn):
    q_start = start_fn(pl.program_id(0), pl.program_id(1))
    n_pairs = kp_ref.shape[1] // 128
    lane = lax.broadcasted_iota(jnp.int32, (1, 128), 1)
    low = lane < 64
    rows = group * tq
    r = lax.broadcasted_iota(jnp.int32, (rows, 1), 0) % tq
    c_prev = lax.broadcasted_iota(jnp.int32, (1, kp_ref.shape[0]), 1)
    c_cur = lax.broadcasted_iota(jnp.int32, (1, tq), 1)
    dist_prev = r + WINDOW - c_prev
    dist_cur = r - c_cur
    in_window = lambda dist: jnp.logical_and(dist >= 0, dist < WINDOW)
    ok_prev = jnp.logical_and(in_window(dist_prev), (q_start - WINDOW + c_prev) >= 0)
    ok_cur = jnp.logical_and(in_window(dist_cur), (q_start + c_cur) >= 0)
    dprev = dist_prev.astype(F32)
    dcur = dist_cur.astype(F32)
    blk = lax.broadcasted_iota(jnp.int32, (rows, 1), 0) // tq

    for p in range(n_pairs):
        sl = slice(p * 128, (p + 1) * 128)
        kp = kp_ref[:, sl].astype(BF16)
        kc = kc_ref[:, sl].astype(BF16)
        vp = vp_ref[:, sl].astype(BF16)
        vc = vc_ref[:, sl].astype(BF16)
        q4 = jnp.concatenate([q_ref[:, (p * group + i) * 128:(p * group + i + 1) * 128]
                              for i in range(group)], axis=0)
        out = None
        for half in range(2):
            keep = low if half == 0 else jnp.logical_not(low)
            zero = jnp.zeros((), BF16)
            slope = jnp.zeros((rows, 1), F32)
            sink = jnp.zeros((rows, 1), F32)
            for i in range(group):
                head = (2 * p + half) * group + i
                slope = jnp.where(blk == i, hs_ref[0, head], slope)
                sink = jnp.where(blk == i, hs_ref[1, head], sink)
            s_prev = _nt(q4, jnp.where(keep, kp, zero)) * scale - slope * dprev
            s_cur = _nt(q4, jnp.where(keep, kc, zero)) * scale - slope * dcur
            s_prev = jnp.where(ok_prev, s_prev, NEG_INF)
            s_cur = jnp.where(ok_cur, s_cur, NEG_INF)
            m = jnp.maximum(jnp.maximum(jnp.max(s_prev, -1, keepdims=True),
                                        jnp.max(s_cur, -1, keepdims=True)), sink)
            e_prev = jnp.exp(s_prev - m)
            e_cur = jnp.exp(s_cur - m)
            denom = jnp.sum(e_prev, -1, keepdims=True) + jnp.sum(e_cur, -1, keepdims=True) + jnp.exp(sink - m)
            o = (_nn((e_prev / denom).astype(BF16), jnp.where(keep, vp, zero))
                 + _nn((e_cur / denom).astype(BF16), jnp.where(keep, vc, zero)))
            out = o if out is None else out + o
        for i in range(group):
            o_ref[:, (p * group + i) * 128:(p * group + i + 1) * 128] = out[i * tq:(i + 1) * tq].astype(o_ref.dtype)


def swa_attn_prompt(hs, q, kv, attn_rows, batch, lp, pad, group, scale):
    nb = lp // BLOCK
    kvw = kv.shape[1] // 2
    d = q.shape[1]
    smem = pl.BlockSpec(memory_space=pltpu.SMEM)
    return pl.pallas_call(
        functools.partial(_swa_kernel, tq=BLOCK, group=group, scale=scale,
                          start_fn=lambda b, j: j * BLOCK - pad),
        grid=(batch, nb),
        in_specs=[smem,
                  pl.BlockSpec((BLOCK, d), lambda b, j: (b * nb + j, 0)),
                  pl.BlockSpec((BLOCK, kvw), lambda b, j: (b * nb + jnp.maximum(j - 1, 0), 0)),
                  pl.BlockSpec((BLOCK, kvw), lambda b, j: (b * nb + j, 0)),
                  pl.BlockSpec((BLOCK, kvw), lambda b, j: (b * nb + jnp.maximum(j - 1, 0), 1)),
                  pl.BlockSpec((BLOCK, kvw), lambda b, j: (b * nb + j, 1))],
        out_specs=pl.BlockSpec((BLOCK, d), lambda b, j: (b * nb + j, 0)),
        out_shape=jax.ShapeDtypeStruct((attn_rows, d), BF16),
        compiler_params=_params(("parallel", "parallel"), 32),
        name="swa_attn_prompt",
    )(hs, q, kv, kv, kv, kv)


def _swa_sample_kernel(hrow_ref, q_ref, kp_ref, kc_ref, vp_ref, vc_ref, o_ref, *, group, scale, past_len):
    seqs, tq, d = q_ref.shape
    n_slabs = d // 128
    n_pairs = kp_ref.shape[2] // 128
    rows = 2 * n_slabs * tq
    lane = lax.broadcasted_iota(jnp.int32, (1, 128), 1)
    low = lane < 64
    t = lax.broadcasted_iota(jnp.int32, (rows, 1), 0) % tq
    c_prev = lax.broadcasted_iota(jnp.int32, (1, WINDOW), 1)
    c_cur = lax.broadcasted_iota(jnp.int32, (1, tq), 1)
    dist_prev = t + WINDOW - c_prev
    dist_cur = t - c_cur
    in_window = lambda dist: jnp.logical_and(dist >= 0, dist < WINDOW)
    ok_prev = jnp.logical_and(in_window(dist_prev), (past_len - WINDOW + c_prev) >= 0)
    ok_cur = in_window(dist_cur)
    slope = hrow_ref[0]
    sink = hrow_ref[1][:, :1]
    bias_prev = slope * dist_prev.astype(F32)
    bias_cur = slope[:, :tq] * dist_cur.astype(F32)
    zero_slab = jnp.zeros((tq, 128), BF16)
    zero = jnp.zeros((), BF16)

    for s in range(seqs):
        blocks = []
        for sidx in range(n_slabs):
            p = sidx // group
            slab = q_ref[s, :, sidx * 128:(sidx + 1) * 128]
            for half in range(2):
                own = jnp.where(low, slab, zero) if half == 0 else jnp.where(low, zero, slab)
                blocks.append(jnp.concatenate([zero_slab] * p + [own] + [zero_slab] * (n_pairs - 1 - p), axis=1))
        qexp = jnp.concatenate(blocks, axis=0)
        kp = kp_ref[s].astype(BF16)
        kc = kc_ref[s].astype(BF16)
        s_prev = jnp.where(ok_prev, _nt(qexp, kp) * scale - bias_prev, NEG_INF)
        s_cur = jnp.where(ok_cur, _nt(qexp, kc) * scale - bias_cur, NEG_INF)
        m = jnp.maximum(jnp.maximum(jnp.max(s_prev, -1, keepdims=True), jnp.max(s_cur, -1, keepdims=True)), sink)
        e_prev = jnp.exp(s_prev - m)
        e_cur = jnp.exp(s_cur - m)
        denom = jnp.sum(e_prev, -1, keepdims=True) + jnp.sum(e_cur, -1, keepdims=True) + jnp.exp(sink - m)
        o = (_nn((e_prev / denom).astype(BF16), vp_ref[s].astype(BF16))
             + _nn((e_cur / denom).astype(BF16), vc_ref[s].astype(BF16)))
        for sidx in range(n_slabs):
            p = sidx // group
            r0 = 2 * sidx * tq
            lo = o[r0:r0 + tq, p * 128:(p + 1) * 128]
            hi = o[r0 + tq:r0 + 2 * tq, p * 128:(p + 1) * 128]
            o_ref[s, :, sidx * 128:(sidx + 1) * 128] = jnp.where(low, lo, hi).astype(o_ref.dtype)


def swa_attn_sample(hrow, q_s, k_cache, v_cache, k_new, v_new, past_len, group, scale, seqs=4):
    db, tq, d = q_s.shape
    kvw = k_cache.shape[2]
    seq = lambda r, w: pl.BlockSpec((seqs, r, w), lambda b: (b, 0, 0))
    return pl.pallas_call(
        functools.partial(_swa_sample_kernel, group=group, scale=scale, past_len=past_len),
        grid=(db // seqs,),
        in_specs=[pl.BlockSpec(hrow.shape, lambda b: (0, 0, 0)),
                  seq(tq, d), seq(WINDOW, kvw), seq(tq, kvw), seq(WINDOW, kvw), seq(tq, kvw)],
        out_specs=seq(tq, d),
        out_shape=jax.ShapeDtypeStruct((db, tq, d), BF16),
        compiler_params=_params(("parallel",), 32),
        name="swa_attn_sample",
    )(hrow, q_s, k_cache, k_new, v_cache, v_new)


def _pair_swap(w):
    return w.reshape(w.shape[:-1] + (w.shape[-1] // 2, 2))[..., ::-1].reshape(w.shape)


def _swa_slab_perm(heads, kv_heads, hd):
    group = heads // kv_heads
    cols = []
    for p in range(kv_heads // 2):
        for i in range(group):
            for half in range(2):
                head = (2 * p + half) * group + i
                cols.extend(range(head * hd, (head + 1) * hd))
    return np.asarray(cols, np.int32)


def kernel(x_prompt, x_sample, cache_mla_ckv, cache_mla_kpe, cache_swa_k, cache_swa_v, page_table, meta_tokens, w_dq, g_q, w_uq, w_dkv, g_kv, w_uk, w_uv, w_o_mla, w_k_shared, w_v_shared, w_q_swa, w_o_swa, sinks, ln_mix_g, ln_mix_b, ln_ffn_g, ln_ffn_b, w_gate, w_up, w_down, w_router, we_gate, we_up, we_down):
    b, seq, d = x_prompt.shape
    db, ds, _ = x_sample.shape
    depth = ln_mix_g.shape[0]
    n_a = w_dq.shape[0]
    n_meta = meta_tokens.shape[0]
    kv_lora, heads, qk_nope = w_uk.shape[1:]
    qk_rope = cache_mla_kpe.shape[3]
    v_head = w_uv.shape[3]
    swa_kv_heads, swa_hd = cache_swa_k.shape[2:]
    swa_heads = w_q_swa.shape[2] // swa_hd
    swa_group = swa_heads // swa_kv_heads
    win_buf = cache_swa_k.shape[1]
    past_len = page_table.shape[1] * cache_mla_ckv.shape[2]
    pad = BLOCK - n_meta
    lp = seq + BLOCK
    n_p = b * lp
    n_s = db * ds
    n = n_p + n_s
    alpha = (2 * depth) ** 0.25
    mla_scale = (qk_nope + qk_rope) ** -0.5
    swa_scale = swa_hd ** -0.5
    assert (qk_nope, qk_rope, v_head, swa_hd, win_buf) == (128, 64, 128, 64, WINDOW)
    assert n_p % 512 == 0 and n_s == 512 and n % 1024 == 0
    sample_block = n_p // n_s
    ds_pad = 16

    hp = jnp.concatenate([jnp.zeros((b, pad, d), x_prompt.dtype),
                          jnp.broadcast_to(meta_tokens.astype(x_prompt.dtype)[None], (b, n_meta, d)),
                          x_prompt], 1)
    x = jnp.concatenate([hp.reshape(n_p, d), x_sample.reshape(n_s, d)], 0)
    xb = x.astype(BF16)

    pos = jnp.concatenate([jnp.tile(jnp.arange(lp, dtype=jnp.int32) - pad, b),
                           jnp.tile(past_len + jnp.arange(ds, dtype=jnp.int32), db)])
    inv = ROPE_THETA ** (-jnp.arange(0, qk_rope, 2, dtype=F32) / qk_rope)
    ang = pos.astype(F32)[:, None] * inv
    cos, sin = jnp.cos(ang), jnp.sin(ang)
    ctab = jnp.pad(jnp.repeat(cos, 2, axis=-1), ((0, 0), (0, 128 - qk_rope)))
    stab = jnp.pad(jnp.stack([-sin, sin], -1).reshape(n, qk_rope), ((0, 0), (0, 128 - qk_rope)))

    pool_kpe_t = jnp.swapaxes(cache_mla_kpe, 2, 3)
    slopes = 2.0 ** (-ALIBI_MAX_BIAS * jnp.arange(1, swa_heads + 1, dtype=F32) / swa_heads)
    perm = _swa_slab_perm(swa_heads, swa_kv_heads, swa_hd)
    head_of_block = perm[::swa_hd] // swa_hd

    ckv_out, kpe_out = [], []
    kv = None
    for l in range(depth):
        g_mix, b_mix = ln_mix_g[l][None], ln_mix_b[l][None]
        if l < n_a:
            w_rope = w_dkv[l][:, kv_lora:]
            w_rope_sw = _pair_swap(w_rope)
            wr = jnp.concatenate([w_rope, w_rope_sw, w_rope_sw, w_rope], -1).astype(BF16)
            wq3 = w_uq[l].reshape(-1, heads, qk_nope + qk_rope)
            wq_rope = wq3[..., qk_nope:]
            wq_rope_sw = _pair_swap(wq_rope)
            wq2 = jnp.concatenate([wq3[..., :qk_nope], wq_rope, wq_rope_sw, wq_rope_sw, wq_rope], -1)
            wq2 = wq2.reshape(-1, heads * 384).astype(BF16)
            wk2 = w_uk[l].reshape(kv_lora, heads * qk_nope).astype(BF16)
            wv2 = w_uv[l].reshape(kv_lora, heads * v_head).astype(BF16)

            cq, ckv, ckvb, kpe = mla_down(xb, w_dq[l].astype(BF16), w_dkv[l][:, :kv_lora].astype(BF16), wr,
                                          g_q[l][None], g_kv[l][None], ctab, stab)
            q_full = mla_qup(cq, wq2, ctab, stab, heads)
            k_full, v_full = mla_kvup(ckvb, kpe, wk2, wv2, n_p, heads)
            attn_p = mla_prompt_attn(q_full, k_full, v_full, n_p, b, lp, heads, pad, mla_scale)

            q_lat = mla_absorb(q_full, wk2, sample_block, n_s, heads).reshape(n_s * heads, kv_lora)
            q_s = q_full[n_p:].reshape(n_s * heads, 256)
            ckv_new = jnp.pad(ckvb[n_p:].reshape(db, ds, kv_lora), ((0, 0), (0, ds_pad - ds), (0, 0)))
            kpe_new = jnp.pad(kpe[n_p:].reshape(db, ds, 128), ((0, 0), (0, ds_pad - ds), (0, 0)))
            o_lat = mla_decode(page_table, q_lat, q_s, ckv_new, kpe_new, cache_mla_ckv, pool_kpe_t,
                               l, heads, mla_scale)
            attn_s = mla_vup(o_lat.reshape(n_s, heads * kv_lora), wv2, heads)
            x, xb = proj_ln(attn_p, attn_s, w_o_mla[l].astype(BF16), x, g_mix, b_mix, alpha)
            ckv_out.append(ckv)
            kpe_out.append(kpe[:, :qk_rope])
        else:
            j = l - n_a
            if kv is None:
                w_kv = jnp.concatenate([w_k_shared, w_v_shared], -1).astype(BF16)
                kv = matmul_rows(xb, w_kv, F32)
                kvw = swa_kv_heads * swa_hd
                k_new = kv[n_p:, :kvw].reshape(db, ds, kvw)
                v_new = kv[n_p:, kvw:].reshape(db, ds, kvw)
                k_new_pad = jnp.pad(k_new, ((0, 0), (0, ds_pad - ds), (0, 0)))
                v_new_pad = jnp.pad(v_new, ((0, 0), (0, ds_pad - ds), (0, 0)))
            w_q = w_q_swa[j].reshape(d, swa_kv_heads // 2, 2, swa_group, swa_hd).transpose(0, 1, 3, 2, 4)
            w_o = w_o_swa[j].reshape(swa_kv_heads // 2, 2, swa_group, swa_hd, d).transpose(0, 2, 1, 3, 4)
            q = matmul_rows(xb, w_q.reshape(d, -1).astype(BF16), BF16)
            hs = jnp.stack([slopes, sinks[j].astype(F32)])
            attn_p = swa_attn_prompt(hs, q, kv, n_p, b, lp, pad, swa_group, swa_scale)
            q_s = jnp.pad(q[n_p:].reshape(db, ds, -1), ((0, 0), (0, ds_pad - ds), (0, 0)))
            hrow = jnp.broadcast_to(jnp.repeat(hs[:, head_of_block], ds_pad, axis=1)[:, :, None],
                                    (2, head_of_block.size * ds_pad, 128))
            attn_s = swa_attn_sample(hrow, q_s, cache_swa_k.reshape(db, win_buf, kvw),
                                     cache_swa_v.reshape(db, win_buf, kvw), k_new_pad, v_new_pad,
                                     past_len, swa_group, swa_scale)
            x, xb = proj_ln(attn_p, attn_s[:, :ds].reshape(n_s, -1), w_o.reshape(-1, d).astype(BF16),
                            x, g_mix, b_mix, alpha)

        g_ffn, b_ffn = ln_ffn_g[l][None], ln_ffn_b[l][None]
        f = l // 2
        if l % 2 == 0:
            h = ffn_gate_up(xb, w_gate, w_up, f)
            x, xb = ffn_down_ln(h, w_down, f, x, g_ffn, b_ffn, alpha)
        else:
            x, xb = moe_ffn_ln(x, w_router, we_gate, we_up, we_down, f, g_ffn, b_ffn, alpha)

    hp_out = x[:n_p].reshape(b, lp, d)
    y_prompt = hp_out[:, BLOCK:]
    y_sample = x[n_p:].reshape(db, ds, d)
    new_ckv_prompt = jnp.stack([c[:n_p].reshape(b, lp, kv_lora)[:, pad:] for c in ckv_out])
    new_kpe_prompt = jnp.stack([c[:n_p].reshape(b, lp, qk_rope)[:, pad:] for c in kpe_out])
    new_ckv_sample = jnp.stack([c[n_p:].reshape(db, ds, kv_lora) for c in ckv_out])
    new_kpe_sample = jnp.stack([c[n_p:].reshape(db, ds, qk_rope) for c in kpe_out])
    wp = min(WINDOW, seq + n_meta)
    k_p = kv[:n_p, :kvw].reshape(b, lp, swa_kv_heads, swa_hd)
    v_p = kv[:n_p, kvw:].reshape(b, lp, swa_kv_heads, swa_hd)
    new_swa_k_prompt = k_p[:, lp - wp:]
    new_swa_v_prompt = v_p[:, lp - wp:]
    new_swa_k_sample = jnp.concatenate([cache_swa_k, k_new.reshape(db, ds, swa_kv_heads, swa_hd)], 1)[:, ds:]
    new_swa_v_sample = jnp.concatenate([cache_swa_v, v_new.reshape(db, ds, swa_kv_heads, swa_hd)], 1)[:, ds:]
    return (y_prompt, y_sample, new_ckv_prompt, new_kpe_prompt, new_ckv_sample, new_kpe_sample,
            new_swa_k_prompt, new_swa_v_prompt, new_swa_k_sample, new_swa_v_sample)
```

```python
import functools

import numpy as np
import jax
import jax.numpy as jnp
from jax import lax
from jax.experimental import pallas as pl
from jax.experimental.pallas import tpu as pltpu

F32 = jnp.float32
BF16 = jnp.bfloat16

BLOCK = 128
WINDOW = 128
ROPE_THETA = 10000.0
ALIBI_MAX_BIAS = 8.0
LN_EPS = 1e-5
RMS_EPS = 1e-6
NEG_INF = -1e30
LOG2E = 1.4426950408889634
N_EXPERT_LANES = 128
ROW_DMA_UNROLL = 8
MOE_TILES_PER_GROUP = 5

V7X_VMEM_BYTES = 64 * 1024 * 1024
VMEM_CEILING = V7X_VMEM_BYTES - 6 * 1024 * 1024
MIB = 1024 * 1024


def _params(semantics, vmem_mib):
    return pltpu.CompilerParams(dimension_semantics=semantics,
                                vmem_limit_bytes=min(int(vmem_mib * MIB), VMEM_CEILING))


def _nt(a, b):
    return lax.dot_general(a, b, (((1,), (1,)), ((), ())), preferred_element_type=F32)


def _nn(a, b):
    return jnp.dot(a, b, preferred_element_type=F32)


def _rms(x, g):
    return x * lax.rsqrt(jnp.mean(jnp.square(x), -1, keepdims=True) + RMS_EPS) * g


def _ln_store(z, g_ref, b_ref, o_ref, ob_ref):
    mu = jnp.mean(z, -1, keepdims=True)
    d = z - mu
    var = jnp.mean(jnp.square(d), -1, keepdims=True)
    y = d * lax.rsqrt(var + LN_EPS) * g_ref[...] + b_ref[...]
    o_ref[...] = y
    ob_ref[...] = y.astype(ob_ref.dtype)


def _silu(g):
    return g * (1.0 / (1.0 + jnp.exp(-g)))


def _mm_kernel(x_ref, w_ref, o_ref):
    o_ref[...] = _nn(x_ref[...], w_ref[...]).astype(o_ref.dtype)


def matmul_rows(x, w, out_dtype, tm=512):
    m, k = x.shape
    n = w.shape[1]
    return pl.pallas_call(
        _mm_kernel,
        grid=(m // tm,),
        in_specs=[pl.BlockSpec((tm, k), lambda i: (i, 0)),
                  pl.BlockSpec((k, n), lambda i: (0, 0))],
        out_specs=pl.BlockSpec((tm, n), lambda i: (i, 0)),
        out_shape=jax.ShapeDtypeStruct((m, n), out_dtype),
        compiler_params=_params(("parallel",), 40),
        name="matmul_rows",
    )(x, w)


def _mla_down_kernel(x_ref, wq_ref, wkv_ref, wr_ref, gq_ref, gkv_ref, c_ref, s_ref,
                     cq_ref, ckv_ref, ckvb_ref, kpe_ref):
    x = x_ref[...].astype(wq_ref.dtype)
    cq_ref[...] = _rms(_nn(x, wq_ref[...]), gq_ref[...]).astype(cq_ref.dtype)
    ckv = _rms(_nn(x, wkv_ref[...]), gkv_ref[...])
    ckv_ref[...] = ckv
    ckvb_ref[...] = ckv.astype(ckvb_ref.dtype)
    r = _nn(x, wr_ref[...])
    half = r.shape[1] // 2
    kpe_ref[...] = r[:, :half] * c_ref[...] + r[:, half:] * s_ref[...]


def mla_down(xb, wq, wkv, wr, gq, gkv, ctab, stab, tm=512):
    n, d = xb.shape
    ql, kl = wq.shape[1], wkv.shape[1]
    row = lambda w: pl.BlockSpec((tm, w), lambda i: (i, 0))
    full = lambda a: pl.BlockSpec(a.shape, lambda i: (0,) * a.ndim)
    return pl.pallas_call(
        _mla_down_kernel,
        grid=(n // tm,),
        in_specs=[row(d), full(wq), full(wkv), full(wr), full(gq), full(gkv), row(128), row(128)],
        out_specs=[row(ql), row(kl), row(kl), row(128)],
        out_shape=[jax.ShapeDtypeStruct((n, ql), BF16), jax.ShapeDtypeStruct((n, kl), F32),
                   jax.ShapeDtypeStruct((n, kl), BF16), jax.ShapeDtypeStruct((n, 128), F32)],
        compiler_params=_params(("parallel",), 40),
        name="mla_down",
    )(xb, wq, wkv, wr, gq, gkv, ctab, stab)


def _mla_qup_kernel(cq_ref, w_ref, c_ref, s_ref, o_ref, *, heads_per_step):
    cq = cq_ref[...]
    c = c_ref[...]
    s = s_ref[...]
    for h in range(heads_per_step):
        acc = _nn(cq, w_ref[:, h * 384:(h + 1) * 384])
        o_ref[:, h * 256:h * 256 + 128] = acc[:, :128].astype(o_ref.dtype)
        o_ref[:, h * 256 + 128:(h + 1) * 256] = (acc[:, 128:256] * c + acc[:, 256:384] * s).astype(o_ref.dtype)


def mla_qup(cq, w, ctab, stab, heads, tm=512, hps=16):
    n, ql = cq.shape
    return pl.pallas_call(
        functools.partial(_mla_qup_kernel, heads_per_step=hps),
        grid=(n // tm, heads // hps),
        in_specs=[pl.BlockSpec((tm, ql), lambda i, j: (i, 0)),
                  pl.BlockSpec((ql, hps * 384), lambda i, j: (0, j)),
                  pl.BlockSpec((tm, 128), lambda i, j: (i, 0)),
                  pl.BlockSpec((tm, 128), lambda i, j: (i, 0))],
        out_specs=pl.BlockSpec((tm, hps * 256), lambda i, j: (i, j)),
        out_shape=jax.ShapeDtypeStruct((n, heads * 256), BF16),
        compiler_params=_params(("parallel", "parallel"), 32),
        name="mla_qup",
    )(cq, w, ctab, stab)


def _mla_kvup_kernel(c_ref, kpe_ref, wk_ref, wv_ref, k_ref, v_ref, *, heads_per_step):
    c = c_ref[...]
    kn = _nn(c, wk_ref[...])
    pe = kpe_ref[...].astype(k_ref.dtype)
    for h in range(heads_per_step):
        k_ref[:, h * 256:h * 256 + 128] = kn[:, h * 128:(h + 1) * 128].astype(k_ref.dtype)
        k_ref[:, h * 256 + 128:(h + 1) * 256] = pe
    v_ref[...] = _nn(c, wv_ref[...]).astype(v_ref.dtype)


def mla_kvup(ckvb, kpe, wk, wv, rows, heads, tm=512, hps=16):
    kl = ckvb.shape[1]
    return pl.pallas_call(
        functools.partial(_mla_kvup_kernel, heads_per_step=hps),
        grid=(rows // tm, heads // hps),
        in_specs=[pl.BlockSpec((tm, kl), lambda i, j: (i, 0)),
                  pl.BlockSpec((tm, 128), lambda i, j: (i, 0)),
                  pl.BlockSpec((kl, hps * 128), lambda i, j: (0, j)),
                  pl.BlockSpec((kl, hps * 128), lambda i, j: (0, j))],
        out_specs=[pl.BlockSpec((tm, hps * 256), lambda i, j: (i, j)),
                   pl.BlockSpec((tm, hps * 128), lambda i, j: (i, j))],
        out_shape=[jax.ShapeDtypeStruct((rows, heads * 256), BF16),
                   jax.ShapeDtypeStruct((rows, heads * 128), BF16)],
        compiler_params=_params(("parallel", "parallel"), 32),
        name="mla_kvup",
    )(ckvb, kpe, wk, wv)


def _mla_prompt_attn_kernel(q_ref, k_ref, v_ref, o_ref, *, blocks, pad, scale):
    scale2 = scale * LOG2E
    for r0, r1 in blocks:
        q = q_ref[r0:r1, :]
        pieces = [(0, min(BLOCK, r0)), (BLOCK, r0), (r0, r1)]
        scores = []
        for c0, c1 in pieces:
            if c1 <= c0:
                continue
            s = _nt(q, k_ref[c0:c1, :]) * scale2
            if c0 < pad:
                col = lax.broadcasted_iota(jnp.int32, s.shape, 1) + c0
                s = jnp.where(col >= pad, s, NEG_INF)
            if c1 > r0:
                row = lax.broadcasted_iota(jnp.int32, s.shape, 0) + r0
                col = lax.broadcasted_iota(jnp.int32, s.shape, 1) + c0
                s = jnp.where(col <= row, s, NEG_INF)
            scores.append((c0, c1, s))
        m = functools.reduce(jnp.maximum, [jnp.max(s, -1, keepdims=True) for _, _, s in scores])
        l = 0.0
        o = 0.0
        for c0, c1, s in scores:
            p = jnp.exp2(s - m)
            l = l + jnp.sum(p, -1, keepdims=True)
            o = o + _nn(p.astype(v_ref.dtype), v_ref[c0:c1, :])
        o_ref[r0:r1, :] = (o / l).astype(o_ref.dtype)


def mla_prompt_attn(q, k, v, n_rows, batch, lp, heads, pad, scale):
    blocks = [(0, BLOCK)] + [(r, r + 256) for r in range(BLOCK, lp, 256)]
    assert blocks[-1][1] == lp
    return pl.pallas_call(
        functools.partial(_mla_prompt_attn_kernel, blocks=tuple(blocks), pad=pad, scale=scale),
        grid=(batch, heads),
        in_specs=[pl.BlockSpec((lp, 256), lambda b, h: (b, h)),
                  pl.BlockSpec((lp, 256), lambda b, h: (b, h)),
                  pl.BlockSpec((lp, 128), lambda b, h: (b, h))],
        out_specs=pl.BlockSpec((lp, 128), lambda b, h: (b, h)),
        out_shape=jax.ShapeDtypeStruct((n_rows, heads * 128), BF16),
        compiler_params=_params(("parallel", "parallel"), 40),
        name="mla_prompt_attn",
    )(q, k, v)


def _mla_absorb_kernel(q_ref, w_ref, o_ref):
    o_ref[...] = _nt(q_ref[:, :128], w_ref[...]).astype(o_ref.dtype)


def mla_absorb(q_full, wk, row_block, rows, heads):
    kl = wk.shape[0]
    return pl.pallas_call(
        _mla_absorb_kernel,
        grid=(heads,),
        in_specs=[pl.BlockSpec((rows, 256), lambda h: (row_block, h)),
                  pl.BlockSpec((kl, 128), lambda h: (0, h))],
        out_specs=pl.BlockSpec((rows, kl), lambda h: (0, h)),
        out_shape=jax.ShapeDtypeStruct((rows, heads * kl), BF16),
        compiler_params=_params(("parallel",), 32),
        name="mla_absorb",
    )(q_full, wk)


def _mla_decode_kernel(pt_ref, ql_ref, q_ref, cn_ref, kn_ref, ckv_hbm, kpe_hbm, o_ref,
                       ckv_buf, kpe_buf, sem, m_sc, l_sc, acc_sc, *, layer, pages, scale, heads, rope):
    j = pl.program_id(1)
    n_chunks = pl.num_programs(1)
    step = pl.program_id(0) * n_chunks + j
    page = ckv_hbm.shape[2]

    def page_copies(page_id, slot, i):
        return (pltpu.make_async_copy(ckv_hbm.at[layer, page_id], ckv_buf.at[slot, pl.ds(i * page, page), :],
                                      sem.at[0, slot]),
                pltpu.make_async_copy(kpe_hbm.at[layer, page_id], kpe_buf.at[slot, i], sem.at[1, slot]))

    def start_chunk(chunk, slot):
        for i in range(pages):
            for copy in page_copies(pt_ref[chunk * pages + i], slot, i):
                copy.start()

    @pl.when(step == 0)
    def _():
        start_chunk(0, 0)

    @pl.when(step + 1 < pl.num_programs(0) * n_chunks)
    def _():
        start_chunk(step + 1, (step + 1) % 2)

    slot = step % 2
    for i in range(pages):
        for copy in page_copies(0, slot, i):
            copy.wait()

    @pl.when(j == 0)
    def _():
        m_sc[...] = jnp.full(m_sc.shape, NEG_INF, F32)
        l_sc[...] = jnp.zeros(l_sc.shape, F32)
        acc_sc[...] = jnp.zeros(acc_sc.shape, F32)

    ql = ql_ref[...]
    qp = q_ref[:, 128:128 + rope]
    scale2 = scale * LOG2E

    def update(s, vals):
        m_prev = m_sc[...]
        m_new = jnp.maximum(m_prev, jnp.max(s, -1, keepdims=True))
        a = jnp.exp2(m_prev - m_new)
        p = jnp.exp2(s - m_new)
        l_sc[...] = a * l_sc[...] + jnp.sum(p, -1, keepdims=True)
        acc_sc[...] = a * acc_sc[...] + _nn(p.astype(vals.dtype), vals)
        m_sc[...] = m_new

    ck = ckv_buf[slot].astype(BF16)
    kpt = jnp.concatenate([kpe_buf[slot, i].astype(BF16) for i in range(pages)], axis=1)
    update((_nt(ql, ck) + _nn(qp, kpt)) * scale2, ck)

    @pl.when(j == pl.num_programs(1) - 1)
    def _():
        cn = cn_ref[...]
        kn = kn_ref[:, :rope].astype(BF16)
        s = (_nt(ql, cn) + _nt(qp, kn)) * scale2
        tok = lax.broadcasted_iota(jnp.int32, s.shape, 0) // heads
        col = lax.broadcasted_iota(jnp.int32, s.shape, 1)
        update(jnp.where(col <= tok, s, NEG_INF), cn)
        o_ref[...] = (acc_sc[...] / l_sc[...]).astype(o_ref.dtype)


def mla_decode(page_table, q_lat, q_s, ckv_new, kpe_new, pool_ckv, pool_kpe_t, layer, heads, scale, pages=32):
    db, n_pages = page_table.shape
    page, kl = pool_ckv.shape[2], pool_ckv.shape[3]
    rope = pool_kpe_t.shape[2]
    rows = q_lat.shape[0] // db
    new_rows = ckv_new.shape[1]
    pages = min(pages, n_pages)
    assert n_pages % pages == 0
    pt = page_table.reshape(-1)
    hbm = pl.BlockSpec(memory_space=pl.ANY)
    in_specs = [pl.BlockSpec((rows, kl), lambda b, j, pt_ref: (b, 0)),
                pl.BlockSpec((rows, 256), lambda b, j, pt_ref: (b, 0)),
                pl.BlockSpec((None, new_rows, kl), lambda b, j, pt_ref: (b, 0, 0)),
                pl.BlockSpec((None, new_rows, 128), lambda b, j, pt_ref: (b, 0, 0)),
                hbm, hbm]
    return pl.pallas_call(
        functools.partial(_mla_decode_kernel, layer=layer, pages=pages, scale=scale, heads=heads, rope=rope),
        grid_spec=pltpu.PrefetchScalarGridSpec(
            num_scalar_prefetch=1,
            grid=(db, n_pages // pages),
            in_specs=in_specs,
            out_specs=pl.BlockSpec((rows, kl), lambda b, j, pt_ref: (b, 0)),
            scratch_shapes=[pltpu.VMEM((2, pages * page, kl), F32), pltpu.VMEM((2, pages, rope, page), F32),
                            pltpu.SemaphoreType.DMA((2, 2)),
                            pltpu.VMEM((rows, 1), F32), pltpu.VMEM((rows, 1), F32),
                            pltpu.VMEM((rows, kl), F32)]),
        out_shape=jax.ShapeDtypeStruct(q_lat.shape, BF16),
        compiler_params=_params(("arbitrary", "arbitrary"), 48),
        name="mla_decode",
    )(pt, q_lat, q_s, ckv_new, kpe_new, pool_ckv, pool_kpe_t)


def mla_vup(o_lat2d, wv, heads):
    rows = o_lat2d.shape[0]
    kl = wv.shape[0]
    return pl.pallas_call(
        _mm_kernel,
        grid=(heads,),
        in_specs=[pl.BlockSpec((rows, kl), lambda h: (0, h)),
                  pl.BlockSpec((kl, 128), lambda h: (0, h))],
        out_specs=pl.BlockSpec((rows, 128), lambda h: (0, h)),
        out_shape=jax.ShapeDtypeStruct((rows, heads * 128), BF16),
        compiler_params=_params(("parallel",), 32),
        name="mla_vup",
    )(o_lat2d, wv)


def _proj_ln_kernel(xp_ref, xs_ref, w_ref, r_ref, g_ref, b_ref, o_ref, ob_ref, *, alpha, prompt_blocks):
    x = jnp.where(pl.program_id(0) < prompt_blocks, xp_ref[...], xs_ref[...])
    z = alpha * r_ref[...] + _nn(x, w_ref[...])
    _ln_store(z, g_ref, b_ref, o_ref, ob_ref)


def proj_ln(x_p, x_s, w, resid, g, b, alpha, tm=512):
    k = x_p.shape[1]
    n, d = resid.shape
    pb = x_p.shape[0] // tm
    row = lambda wd: pl.BlockSpec((tm, wd), lambda i: (i, 0))
    full = lambda a: pl.BlockSpec(a.shape, lambda i: (0,) * a.ndim)
    return pl.pallas_call(
        functools.partial(_proj_ln_kernel, alpha=alpha, prompt_blocks=pb),
        grid=(n // tm,),
        in_specs=[pl.BlockSpec((tm, k), lambda i: (jnp.minimum(i, pb - 1), 0)),
                  pl.BlockSpec((tm, k), lambda i: (jnp.maximum(i - pb, 0), 0)),
                  full(w), row(d), full(g), full(b)],
        out_specs=[row(d), row(d)],
        out_shape=[jax.ShapeDtypeStruct((n, d), F32), jax.ShapeDtypeStruct((n, d), BF16)],
        compiler_params=_params(("parallel",), 48),
        name="proj_ln",
    )(x_p, x_s, w, resid, g, b)


def _ffn_gu_kernel(x_ref, wg_ref, wu_ref, o_ref, wgb, wub):
    @pl.when(pl.program_id(1) == 0)
    def _():
        wgb[...] = wg_ref[...].astype(BF16)
        wub[...] = wu_ref[...].astype(BF16)

    x = x_ref[...]
    g = _nn(x, wgb[...])
    u = _nn(x, wub[...])
    o_ref[...] = (_silu(g) * u).astype(o_ref.dtype)


def ffn_gate_up(xb, w_gate, w_up, f, tm=1024, tn=512):
    n, d = xb.shape
    dff = w_gate.shape[2]
    wspec = pl.BlockSpec((None, d, tn), lambda j, i: (f, 0, j))
    return pl.pallas_call(
        _ffn_gu_kernel,
        grid=(dff // tn, n // tm),
        in_specs=[pl.BlockSpec((tm, d), lambda j, i: (i, 0)), wspec, wspec],
        out_specs=pl.BlockSpec((tm, tn), lambda j, i: (i, j)),
        out_shape=jax.ShapeDtypeStruct((n, dff), BF16),
        scratch_shapes=[pltpu.VMEM((d, tn), BF16), pltpu.VMEM((d, tn), BF16)],
        compiler_params=_params(("arbitrary", "arbitrary"), 48),
        name="ffn_gate_up",
    )(xb, w_gate, w_up)


def _ffn_down_ln_kernel(h_ref, w_ref, r_ref, g_ref, b_ref, o_ref, ob_ref, acc, *, alpha):
    k = pl.program_id(1)

    @pl.when(k == 0)
    def _():
        acc[...] = jnp.zeros(acc.shape, F32)

    acc[...] += _nn(h_ref[...], w_ref[...].astype(BF16))

    @pl.when(k == pl.num_programs(1) - 1)
    def _():
        _ln_store(alpha * r_ref[...] + acc[...], g_ref, b_ref, o_ref, ob_ref)


def ffn_down_ln(h, w_down, f, resid, g, b, alpha, tm=1024, tk=512):
    n, dff = h.shape
    d = w_down.shape[2]
    row = pl.BlockSpec((tm, d), lambda i, k: (i, 0), pipeline_mode=pl.Buffered(1))
    vec = lambda a: pl.BlockSpec(a.shape, lambda i, k: (0,) * a.ndim)
    return pl.pallas_call(
        functools.partial(_ffn_down_ln_kernel, alpha=alpha),
        grid=(n // tm, dff // tk),
        in_specs=[pl.BlockSpec((tm, tk), lambda i, k: (i, k)),
                  pl.BlockSpec((None, tk, d), lambda i, k: (f, k, 0)),
                  row, vec(g), vec(b)],
        out_specs=[row, row],
        out_shape=[jax.ShapeDtypeStruct((n, d), F32), jax.ShapeDtypeStruct((n, d), BF16)],
        scratch_shapes=[pltpu.VMEM((tm, d), F32)],
        compiler_params=_params(("parallel", "arbitrary"), 56),
        name="ffn_down_ln",
    )(h, w_down, resid, g, b)


def _router_kernel(x_ref, w_ref, info_ref, cnt_ref, x3_ref, carry, *, n_experts):
    i = pl.program_id(0)

    @pl.when(i == 0)
    def _():
        carry[...] = jnp.zeros(carry.shape, F32)

    logits = jnp.dot(x_ref[...], w_ref[...], preferred_element_type=F32, precision=lax.Precision.HIGHEST)
    tm = logits.shape[0]
    lane = lax.broadcasted_iota(jnp.int32, logits.shape, 1).astype(F32)
    lowest = float(np.finfo(np.float32).min)
    lg = jnp.where(lane < n_experts, logits, lowest)
    v1 = jnp.max(lg, -1, keepdims=True)
    i1 = jnp.min(jnp.where(lg == v1, lane, float(N_EXPERT_LANES)), -1, keepdims=True)
    lg2 = jnp.where(lane == i1, lowest, lg)
    v2 = jnp.max(lg2, -1, keepdims=True)
    i2 = jnp.min(jnp.where(lg2 == v2, lane, float(N_EXPERT_LANES)), -1, keepdims=True)
    e2 = jnp.exp(v2 - v1)
    w1 = 1.0 / (1.0 + e2)
    w2 = e2 / (1.0 + e2)
    sel1 = lane == i1
    sel2 = lane == i2
    onehot = jnp.where(sel1, 1.0, jnp.where(sel2, 1.0, 0.0))
    r = lax.broadcasted_iota(jnp.int32, (tm, tm), 0)
    c = lax.broadcasted_iota(jnp.int32, (tm, tm), 1)
    lower = jnp.where(c < r, 1.0, 0.0).astype(BF16)
    before = _nn(lower, onehot.astype(BF16)) + carry[...]
    rank1 = jnp.sum(jnp.where(sel1, before, 0.0), -1, keepdims=True)
    rank2 = jnp.sum(jnp.where(sel2, before, 0.0), -1, keepdims=True)
    total = carry[...] + jnp.sum(onehot, 0, keepdims=True)
    carry[...] = total
    cols = (i1, i2, rank1, rank2, w1, w2)
    info = jnp.zeros(logits.shape, F32)
    for k, val in enumerate(cols):
        info = jnp.where(lane == k, val, info)
    info_ref[...] = info
    cnt_ref[...] = jnp.broadcast_to(total, cnt_ref.shape)
    chunks = x_ref.shape[1] // 128
    for j in range(chunks):
        x3_ref[pl.ds(j, tm, stride=chunks), :] = x_ref[:, j * 128:(j + 1) * 128]


def moe_router(x, w_pad, n_experts, tm=512):
    n, d = x.shape
    chunks = d // 128
    return pl.pallas_call(
        functools.partial(_router_kernel, n_experts=n_experts),
        grid=(n // tm,),
        in_specs=[pl.BlockSpec((tm, d), lambda i: (i, 0)),
                  pl.BlockSpec(w_pad.shape, lambda i: (0, 0))],
        out_specs=[pl.BlockSpec((tm, N_EXPERT_LANES), lambda i: (i, 0)),
                   pl.BlockSpec((8, N_EXPERT_LANES), lambda i: (0, 0)),
                   pl.BlockSpec((tm * chunks, 128), lambda i: (i, 0))],
        out_shape=[jax.ShapeDtypeStruct((n, N_EXPERT_LANES), F32),
                   jax.ShapeDtypeStruct((8, N_EXPERT_LANES), F32),
                   jax.ShapeDtypeStruct((n * chunks, 128), F32)],
        scratch_shapes=[pltpu.VMEM((1, N_EXPERT_LANES), F32)],
        compiler_params=_params(("arbitrary",), 40),
        name="moe_router",
    )(x, w_pad)


def _moe_gather_kernel(tok_ref, x3_hbm, o_ref, buf, sem):
    tm = o_ref.shape[0]
    chunks = x3_hbm.shape[1]
    i = pl.program_id(0)

    def row_copy(token, slot, r):
        return pltpu.make_async_copy(x3_hbm.at[token], buf.at[slot, pl.ds(r * chunks, chunks), :], sem.at[slot])

    def start_tile(tile, slot):
        def body(g, carry):
            for u in range(ROW_DMA_UNROLL):
                r = g * ROW_DMA_UNROLL + u
                row_copy(tok_ref[tile * tm + r], slot, r).start(priority=u % 2)
            return carry
        lax.fori_loop(0, tm // ROW_DMA_UNROLL, body, 0)

    @pl.when(i == 0)
    def _():
        start_tile(0, 0)

    @pl.when(i + 1 < pl.num_programs(0))
    def _():
        start_tile(i + 1, (i + 1) % 2)

    slot = i % 2

    def wait(r, carry):
        row_copy(0, slot, r).wait()
        return carry

    lax.fori_loop(0, tm, wait, 0, unroll=8)
    for j in range(chunks):
        o_ref[:, j * 128:(j + 1) * 128] = buf[slot, pl.ds(j, tm, stride=chunks), :].astype(o_ref.dtype)


def moe_gather(token_of_row, x3, tm=256):
    rows = token_of_row.shape[0]
    chunks = x3.shape[1]
    return pl.pallas_call(
        _moe_gather_kernel,
        grid_spec=pltpu.PrefetchScalarGridSpec(
            num_scalar_prefetch=1,
            grid=(rows // tm,),
            in_specs=[pl.BlockSpec(memory_space=pl.ANY)],
            out_specs=pl.BlockSpec((tm, chunks * 128), lambda i, tok: (i, 0)),
            scratch_shapes=[pltpu.VMEM((2, tm * chunks, 128), F32), pltpu.SemaphoreType.DMA((2,))]),
        out_shape=jax.ShapeDtypeStruct((rows, chunks * 128), BF16),
        compiler_params=_params(("arbitrary",), 32),
        name="moe_gather",
    )(token_of_row, x3)


def _moe_experts_kernel(se_ref, st_ref, sn_ref, sz_ref, xs_hbm, wg_hbm, wu_hbm, wd_hbm, y_hbm,
                        x_buf, acc, wg_buf, wu_buf, wd_buf, sem, *, layer, tile, tiles_per_group):
    s = pl.program_id(0)
    expert = se_ref[s]
    n_compute = sn_ref[s]
    n_zero = sz_ref[s]
    first_tile = st_ref[s]
    tf = wg_buf.shape[2]
    n_f = wg_hbm.shape[3] // tf
    row_sem, out_sem, wg_sem, wu_sem, wd_sem = range(5)

    def rows(t):
        return pl.ds(t * tile, tile)

    def tile_in(t):
        return pltpu.make_async_copy(xs_hbm.at[pl.ds((first_tile + t) * tile, tile), :], x_buf.at[rows(t), :],
                                     sem.at[row_sem, 0])

    def tile_out(t):
        return pltpu.make_async_copy(acc.at[rows(t), :], y_hbm.at[pl.ds((first_tile + t) * tile, tile), :],
                                     sem.at[out_sem, 0])

    def weight_copies(f, slot):
        block = pl.ds(pl.multiple_of(f * tf, tf), tf)
        return (pltpu.make_async_copy(wg_hbm.at[layer, expert, :, block], wg_buf.at[slot], sem.at[wg_sem, slot]),
                pltpu.make_async_copy(wu_hbm.at[layer, expert, :, block], wu_buf.at[slot], sem.at[wu_sem, slot]),
                pltpu.make_async_copy(wd_hbm.at[layer, expert, block, :], wd_buf.at[slot], sem.at[wd_sem, slot]))

    def for_tiles(count, fn):
        for t in range(tiles_per_group):
            @pl.when(t < count)
            def _(t=t):
                fn(t)

    @pl.when(n_compute > 0)
    def _():
        for copy in weight_copies(0, 0):
            copy.start()
        for_tiles(n_compute, lambda t: tile_in(t).start())
        acc[...] = jnp.zeros(acc.shape, F32)
        for_tiles(n_compute, lambda t: tile_in(t).wait())

        def d_ff_block(f, carry):
            slot = f % 2

            @pl.when(f + 1 < n_f)
            def _():
                for copy in weight_copies(f + 1, 1 - slot):
                    copy.start()

            for copy in weight_copies(f, slot):
                copy.wait()
            w_gu = jnp.concatenate([wg_buf[slot].astype(BF16), wu_buf[slot].astype(BF16)], axis=1)
            w_d = wd_buf[slot].astype(BF16)

            def compute(t):
                gu = _nn(x_buf[rows(t), :], w_gu)
                h = (_silu(gu[:, :tf]) * gu[:, tf:]).astype(BF16)
                acc[rows(t), :] += _nn(h, w_d)

            for_tiles(n_compute, compute)
            return carry

        lax.fori_loop(0, n_f, d_ff_block, 0)
        for_tiles(n_compute, lambda t: tile_out(t).start())
        for_tiles(n_compute, lambda t: tile_out(t).wait())

    @pl.when(n_zero > 0)
    def _():
        acc[...] = jnp.zeros(acc.shape, F32)
        for_tiles(n_zero, lambda t: tile_out(t).start())
        for_tiles(n_zero, lambda t: tile_out(t).wait())


def moe_experts(sup_expert, sup_tile, sup_compute, sup_zero, xs, we_gate, we_up, we_down, layer,
                tile, tiles_per_group, tf=256):
    rows, d = xs.shape
    dff = we_gate.shape[3]
    n_groups = sup_expert.shape[0]
    assert dff % tf == 0
    hbm = pl.BlockSpec(memory_space=pl.ANY)
    return pl.pallas_call(
        functools.partial(_moe_experts_kernel, layer=layer, tile=tile, tiles_per_group=tiles_per_group),
        grid_spec=pltpu.PrefetchScalarGridSpec(
            num_scalar_prefetch=4,
            grid=(n_groups,),
            in_specs=[hbm, hbm, hbm, hbm],
            out_specs=hbm,
            scratch_shapes=[pltpu.VMEM((tiles_per_group * tile, d), BF16),
                            pltpu.VMEM((tiles_per_group * tile, d), F32),
                            pltpu.VMEM((2, d, tf), F32), pltpu.VMEM((2, d, tf), F32), pltpu.VMEM((2, tf, d), F32),
                            pltpu.SemaphoreType.DMA((5, 2))]),
        out_shape=jax.ShapeDtypeStruct((rows, d), F32),
        compiler_params=_params(("arbitrary",), 58),
        name="moe_experts",
    )(sup_expert, sup_tile, sup_compute, sup_zero, xs, we_gate, we_up, we_down)


def _moe_combine_ln_kernel(p1_ref, p2_ref, y_hbm, info_ref, r_ref, g_ref, b_ref, o_ref, ob_ref,
                           buf, sem, *, alpha):
    tm = o_ref.shape[0]
    i = pl.program_id(0)
    pos_refs = (p1_ref, p2_ref)

    def row_copy(src_row, slot, which, r):
        return pltpu.make_async_copy(y_hbm.at[pl.ds(src_row, 1), :], buf.at[slot, which, pl.ds(r, 1), :],
                                     sem.at[slot, which])

    def start_tile(tile, slot):
        def body(g, carry):
            for u in range(ROW_DMA_UNROLL):
                r = g * ROW_DMA_UNROLL + u
                for which in range(2):
                    row_copy(pos_refs[which][tile * tm + r], slot, which, r).start(priority=which)
            return carry
        lax.fori_loop(0, tm // ROW_DMA_UNROLL, body, 0)

    @pl.when(i == 0)
    def _():
        start_tile(0, 0)

    @pl.when(i + 1 < pl.num_programs(0))
    def _():
        start_tile(i + 1, (i + 1) % 2)

    slot = i % 2

    def wait(r, carry):
        for which in range(2):
            row_copy(0, slot, which, r).wait()
        return carry

    lax.fori_loop(0, tm, wait, 0, unroll=ROW_DMA_UNROLL)
    info = info_ref[...]
    f = info[:, 4:5] * buf[slot, 0] + info[:, 5:6] * buf[slot, 1]
    _ln_store(alpha * r_ref[...] + f, g_ref, b_ref, o_ref, ob_ref)


def moe_combine_ln(pos1, pos2, y, info, resid, g, b, alpha, tm=256):
    n, d = resid.shape
    row = lambda w: pl.BlockSpec((tm, w), lambda i, p1, p2: (i, 0))
    vec = lambda a: pl.BlockSpec(a.shape, lambda i, p1, p2: (0,) * a.ndim)
    return pl.pallas_call(
        functools.partial(_moe_combine_ln_kernel, alpha=alpha),
        grid_spec=pltpu.PrefetchScalarGridSpec(
            num_scalar_prefetch=2,
            grid=(n // tm,),
            in_specs=[pl.BlockSpec(memory_space=pl.ANY), row(N_EXPERT_LANES), row(d), vec(g), vec(b)],
            out_specs=[row(d), row(d)],
            scratch_shapes=[pltpu.VMEM((2, 2, tm, d), F32), pltpu.SemaphoreType.DMA((2, 2))]),
        out_shape=[jax.ShapeDtypeStruct((n, d), F32), jax.ShapeDtypeStruct((n, d), BF16)],
        compiler_params=_params(("arbitrary",), 40),
        name="moe_combine_ln",
    )(pos1, pos2, y, info, resid, g, b)


def moe_ffn_ln(x, w_router, we_gate, we_up, we_down, f, g, b, alpha, tile=512):
    n, d = x.shape
    n_experts = w_router.shape[2]
    w_pad = jnp.pad(w_router[f], ((0, 0), (0, N_EXPERT_LANES - n_experts)))
    info, cnt, x3 = moe_router(x, w_pad, n_experts)
    e1 = info[:, 0].astype(jnp.int32)
    e2 = info[:, 1].astype(jnp.int32)
    counts = cnt[0, :n_experts].astype(jnp.int32)
    group = (counts + tile - 1) // tile * tile
    ends = jnp.cumsum(group)
    starts = ends - group
    pos1 = starts[e1] + info[:, 2].astype(jnp.int32)
    pos2 = starts[e2] + info[:, 3].astype(jnp.int32)
    rows = 2 * n + n_experts * tile
    tok = jnp.arange(n, dtype=jnp.int32)
    token_of_row = jnp.zeros((rows,), jnp.int32).at[jnp.concatenate([pos1, pos2])].set(
        jnp.concatenate([tok, tok]), unique_indices=True)
    tpg = MOE_TILES_PER_GROUP
    n_tiles = rows // tile
    n_valid = ends[-1] // tile
    tiles_e = group // tile
    groups_e = (tiles_e + tpg - 1) // tpg
    g_end = jnp.cumsum(groups_e)
    g_start = g_end - groups_e
    n_compute = g_end[-1]
    n_groups = (n_tiles + (tpg - 1) * (n_experts + 1)) // tpg + 1
    sidx = jnp.arange(n_groups, dtype=jnp.int32)
    e_of = jnp.minimum(jnp.sum((sidx[:, None] >= g_end[None, :]).astype(jnp.int32), -1), n_experts - 1)
    k = sidx - g_start[e_of]
    is_compute = sidx < n_compute
    sup_compute = jnp.where(is_compute, jnp.clip(tiles_e[e_of] - k * tpg, 0, tpg), 0)
    zero_tile = n_valid + (sidx - n_compute) * tpg
    sup_zero = jnp.where(is_compute, 0, jnp.clip(n_tiles - zero_tile, 0, tpg))
    sup_tile = jnp.where(is_compute, starts[e_of] // tile + k * tpg, jnp.where(sup_zero > 0, zero_tile, 0))
    sup_expert = jnp.where(is_compute, e_of, e_of[n_compute - 1])
    as_i32 = lambda a: a.astype(jnp.int32)

    xs = moe_gather(token_of_row, x3.reshape(n, d // 128, 128))
    y = moe_experts(as_i32(sup_expert), as_i32(sup_tile), as_i32(sup_compute), as_i32(sup_zero), xs,
                    we_gate, we_up, we_down, f, tile, tpg)
    return moe_combine_ln(pos1, pos2, y, info, x, g, b, alpha)


def _swa_kernel(hs_ref, q_ref, kp_ref, kc_ref, vp_ref, vc_ref, o_ref, *, tq, group, scale, start_fn):
    q_start = start_fn(pl.program_id(0), pl.program_id(1))
    n_pairs = kp_ref.shape[1] // 128
    lane = lax.broadcasted_iota(jnp.int32, (1, 128), 1)
    low = lane < 64
    rows = group * tq
    r = lax.broadcasted_iota(jnp.int32, (rows, 1), 0) % tq
    c_prev = lax.broadcasted_iota(jnp.int32, (1, kp_ref.shape[0]), 1)
    c_cur = lax.broadcasted_iota(jnp.int32, (1, tq), 1)
    dist_prev = r + WINDOW - c_prev
    dist_cur = r - c_cur
    in_window = lambda dist: jnp.logical_and(dist >= 0, dist < WINDOW)
    ok_prev = jnp.logical_and(in_window(dist_prev), (q_start - WINDOW + c_prev) >= 0)
    ok_cur = jnp.logical_and(in_window(dist_cur), (q_start + c_cur) >= 0)
    dprev = dist_prev.astype(F32)
    dcur = dist_cur.astype(F32)
    blk = lax.broadcasted_iota(jnp.int32, (rows, 1), 0) // tq

    for p in range(n_pairs):
        sl = slice(p * 128, (p + 1) * 128)
        kp = kp_ref[:, sl].astype(BF16)
        kc = kc_ref[:, sl].astype(BF16)
        vp = vp_ref[:, sl].astype(BF16)
        vc = vc_ref[:, sl].astype(BF16)
        q4 = jnp.concatenate([q_ref[:, (p * group + i) * 128:(p * group + i + 1) * 128]
                              for i in range(group)], axis=0)
        out = None
        for half in range(2):
            keep = low if half == 0 else jnp.logical_not(low)
            zero = jnp.zeros((), BF16)
            slope = jnp.zeros((rows, 1), F32)
            sink = jnp.zeros((rows, 1), F32)
            for i in range(group):
                head = (2 * p + half) * group + i
                slope = jnp.where(blk == i, hs_ref[0, head], slope)
                sink = jnp.where(blk == i, hs_ref[1, head], sink)
            s_prev = _nt(q4, jnp.where(keep, kp, zero)) * scale - slope * dprev
            s_cur = _nt(q4, jnp.where(keep, kc, zero)) * scale - slope * dcur
            s_prev = jnp.where(ok_prev, s_prev, NEG_INF)
            s_cur = jnp.where(ok_cur, s_cur, NEG_INF)
            m = jnp.maximum(jnp.maximum(jnp.max(s_prev, -1, keepdims=True),
                                        jnp.max(s_cur, -1, keepdims=True)), sink)
            e_prev = jnp.exp(s_prev - m)
            e_cur = jnp.exp(s_cur - m)
            denom = jnp.sum(e_prev, -1, keepdims=True) + jnp.sum(e_cur, -1, keepdims=True) + jnp.exp(sink - m)
            o = (_nn((e_prev / denom).astype(BF16), jnp.where(keep, vp, zero))
                 + _nn((e_cur / denom).astype(BF16), jnp.where(keep, vc, zero)))
            out = o if out is None else out + o
        for i in range(group):
            o_ref[:, (p * group + i) * 128:(p * group + i + 1) * 128] = out[i * tq:(i + 1) * tq].astype(o_ref.dtype)


def swa_attn_prompt(hs, q, kv, attn_rows, batch, lp, pad, group, scale):
    nb = lp // BLOCK
    kvw = kv.shape[1] // 2
    d = q.shape[1]
    smem = pl.BlockSpec(memory_space=pltpu.SMEM)
    return pl.pallas_call(
        functools.partial(_swa_kernel, tq=BLOCK, group=group, scale=scale,
                          start_fn=lambda b, j: j * BLOCK - pad),
        grid=(batch, nb),
        in_specs=[smem,
                  pl.BlockSpec((BLOCK, d), lambda b, j: (b * nb + j, 0)),
                  pl.BlockSpec((BLOCK, kvw), lambda b, j: (b * nb + jnp.maximum(j - 1, 0), 0)),
                  pl.BlockSpec((BLOCK, kvw), lambda b, j: (b * nb + j, 0)),
                  pl.BlockSpec((BLOCK, kvw), lambda b, j: (b * nb + jnp.maximum(j - 1, 0), 1)),
                  pl.BlockSpec((BLOCK, kvw), lambda b, j: (b * nb + j, 1))],
        out_specs=pl.BlockSpec((BLOCK, d), lambda b, j: (b * nb + j, 0)),
        out_shape=jax.ShapeDtypeStruct((attn_rows, d), BF16),
        compiler_params=_params(("parallel", "parallel"), 32),
        name="swa_attn_prompt",
    )(hs, q, kv, kv, kv, kv)


def _swa_sample_kernel(hrow_ref, q_ref, kp_ref, kc_ref, vp_ref, vc_ref, o_ref, *, group, scale, past_len):
    seqs, tq, d = q_ref.shape
    n_slabs = d // 128
    n_pairs = kp_ref.shape[2] // 128
    rows = 2 * n_slabs * tq
    lane = lax.broadcasted_iota(jnp.int32, (1, 128), 1)
    low = lane < 64
    t = lax.broadcasted_iota(jnp.int32, (rows, 1), 0) % tq
    c_prev = lax.broadcasted_iota(jnp.int32, (1, WINDOW), 1)
    c_cur = lax.broadcasted_iota(jnp.int32, (1, tq), 1)
    dist_prev = t + WINDOW - c_prev
    dist_cur = t - c_cur
    in_window = lambda dist: jnp.logical_and(dist >= 0, dist < WINDOW)
    ok_prev = jnp.logical_and(in_window(dist_prev), (past_len - WINDOW + c_prev) >= 0)
    ok_cur = in_window(dist_cur)
    slope = hrow_ref[0]
    sink = hrow_ref[1][:, :1]
    bias_prev = slope * dist_prev.astype(F32)
    bias_cur = slope[:, :tq] * dist_cur.astype(F32)
    zero_slab = jnp.zeros((tq, 128), BF16)
    zero = jnp.zeros((), BF16)

    for s in range(seqs):
        blocks = []
        for sidx in range(n_slabs):
            p = sidx // group
            slab = q_ref[s, :, sidx * 128:(sidx + 1) * 128]
            for half in range(2):
                own = jnp.where(low, slab, zero) if half == 0 else jnp.where(low, zero, slab)
                blocks.append(jnp.concatenate([zero_slab] * p + [own] + [zero_slab] * (n_pairs - 1 - p), axis=1))
        qexp = jnp.concatenate(blocks, axis=0)
        kp = kp_ref[s].astype(BF16)
        kc = kc_ref[s].astype(BF16)
        s_prev = jnp.where(ok_prev, _nt(qexp, kp) * scale - bias_prev, NEG_INF)
        s_cur = jnp.where(ok_cur, _nt(qexp, kc) * scale - bias_cur, NEG_INF)
        m = jnp.maximum(jnp.maximum(jnp.max(s_prev, -1, keepdims=True), jnp.max(s_cur, -1, keepdims=True)), sink)
        e_prev = jnp.exp(s_prev - m)
        e_cur = jnp.exp(s_cur - m)
        denom = jnp.sum(e_prev, -1, keepdims=True) + jnp.sum(e_cur, -1, keepdims=True) + jnp.exp(sink - m)
        o = (_nn((e_prev / denom).astype(BF16), vp_ref[s].astype(BF16))
             + _nn((e_cur / denom).astype(BF16), vc_ref[s].astype(BF16)))
        for sidx in range(n_slabs):
            p = sidx // group
            r0 = 2 * sidx * tq
            lo = o[r0:r0 + tq, p * 128:(p + 1) * 128]
            hi = o[r0 + tq:r0 + 2 * tq, p * 128:(p + 1) * 128]
            o_ref[s, :, sidx * 128:(sidx + 1) * 128] = jnp.where(low, lo, hi).astype(o_ref.dtype)


def swa_attn_sample(hrow, q_s, k_cache, v_cache, k_new, v_new, past_len, group, scale, seqs=4):
    db, tq, d = q_s.shape
    kvw = k_cache.shape[2]
    seq = lambda r, w: pl.BlockSpec((seqs, r, w), lambda b: (b, 0, 0))
    return pl.pallas_call(
        functools.partial(_swa_sample_kernel, group=group, scale=scale, past_len=past_len),
        grid=(db // seqs,),
        in_specs=[pl.BlockSpec(hrow.shape, lambda b: (0, 0, 0)),
                  seq(tq, d), seq(WINDOW, kvw), seq(tq, kvw), seq(WINDOW, kvw), seq(tq, kvw)],
        out_specs=seq(tq, d),
        out_shape=jax.ShapeDtypeStruct((db, tq, d), BF16),
        compiler_params=_params(("parallel",), 32),
        name="swa_attn_sample",
    )(hrow, q_s, k_cache, k_new, v_cache, v_new)


def _pair_swap(w):
    return w.reshape(w.shape[:-1] + (w.shape[-1] // 2, 2))[..., ::-1].reshape(w.shape)


def _swa_slab_perm(heads, kv_heads, hd):
    group = heads // kv_heads
    cols = []
    for p in range(kv_heads // 2):
        for i in range(group):
            for half in range(2):
                head = (2 * p + half) * group + i
                cols.extend(range(head * hd, (head + 1) * hd))
    return np.asarray(cols, np.int32)


def kernel(x_prompt, x_sample, cache_mla_ckv, cache_mla_kpe, cache_swa_k, cache_swa_v, page_table, meta_tokens, w_dq, g_q, w_uq, w_dkv, g_kv, w_uk, w_uv, w_o_mla, w_k_shared, w_v_shared, w_q_swa, w_o_swa, sinks, ln_mix_g, ln_mix_b, ln_ffn_g, ln_ffn_b, w_gate, w_up, w_down, w_router, we_gate, we_up, we_down):
    b, seq, d = x_prompt.shape
    db, ds, _ = x_sample.shape
    depth = ln_mix_g.shape[0]
    n_a = w_dq.shape[0]
    n_meta = meta_tokens.shape[0]
    kv_lora, heads, qk_nope = w_uk.shape[1:]
    qk_rope = cache_mla_kpe.shape[3]
    v_head = w_uv.shape[3]
    swa_kv_heads, swa_hd = cache_swa_k.shape[2:]
    swa_heads = w_q_swa.shape[2] // swa_hd
    swa_group = swa_heads // swa_kv_heads
    win_buf = cache_swa_k.shape[1]
    past_len = page_table.shape[1] * cache_mla_ckv.shape[2]
    pad = BLOCK - n_meta
    lp = seq + BLOCK
    n_p = b * lp
    n_s = db * ds
    n = n_p + n_s
    alpha = (2 * depth) ** 0.25
    mla_scale = (qk_nope + qk_rope) ** -0.5
    swa_scale = swa_hd ** -0.5
    assert (qk_nope, qk_rope, v_head, swa_hd, win_buf) == (128, 64, 128, 64, WINDOW)
    assert n_p % 512 == 0 and n_s == 512 and n % 1024 == 0
    sample_block = n_p // n_s
    ds_pad = 16

    front = jnp.concatenate([jnp.zeros((pad, d), x_prompt.dtype), meta_tokens.astype(x_prompt.dtype)], 0)
    x = jnp.concatenate([piece for i in range(b) for piece in (front, x_prompt[i])] + [x_sample.reshape(n_s, d)], 0)
    xb = x

    pos = jnp.concatenate([jnp.tile(jnp.arange(lp, dtype=jnp.int32) - pad, b),
                           jnp.tile(past_len + jnp.arange(ds, dtype=jnp.int32), db)])
    inv = ROPE_THETA ** (-jnp.arange(0, qk_rope, 2, dtype=F32) / qk_rope)
    ang = pos.astype(F32)[:, None] * inv
    cos, sin = jnp.cos(ang), jnp.sin(ang)
    ctab = jnp.pad(jnp.repeat(cos, 2, axis=-1), ((0, 0), (0, 128 - qk_rope)))
    stab = jnp.pad(jnp.stack([-sin, sin], -1).reshape(n, qk_rope), ((0, 0), (0, 128 - qk_rope)))

    pool_kpe_t = jnp.swapaxes(cache_mla_kpe, 2, 3)
    slopes = 2.0 ** (-ALIBI_MAX_BIAS * jnp.arange(1, swa_heads + 1, dtype=F32) / swa_heads)
    perm = _swa_slab_perm(swa_heads, swa_kv_heads, swa_hd)
    head_of_block = perm[::swa_hd] // swa_hd

    ckv_out, kpe_out = [], []
    kv = None
    for l in range(depth):
        g_mix, b_mix = ln_mix_g[l][None], ln_mix_b[l][None]
        if l < n_a:
            w_rope = w_dkv[l][:, kv_lora:]
            w_rope_sw = _pair_swap(w_rope)
            wr = jnp.concatenate([w_rope, w_rope_sw, w_rope_sw, w_rope], -1).astype(BF16)
            wq3 = w_uq[l].reshape(-1, heads, qk_nope + qk_rope)
            wq_rope = wq3[..., qk_nope:]
            wq_rope_sw = _pair_swap(wq_rope)
            wq2 = jnp.concatenate([wq3[..., :qk_nope], wq_rope, wq_rope_sw, wq_rope_sw, wq_rope], -1)
            wq2 = wq2.reshape(-1, heads * 384).astype(BF16)
            wk2 = w_uk[l].reshape(kv_lora, heads * qk_nope).astype(BF16)
            wv2 = w_uv[l].reshape(kv_lora, heads * v_head).astype(BF16)

            cq, ckv, ckvb, kpe = mla_down(xb, w_dq[l].astype(BF16), w_dkv[l][:, :kv_lora].astype(BF16), wr,
                                          g_q[l][None], g_kv[l][None], ctab, stab)
            q_full = mla_qup(cq, wq2, ctab, stab, heads)
            k_full, v_full = mla_kvup(ckvb, kpe, wk2, wv2, n_p, heads)
            attn_p = mla_prompt_attn(q_full, k_full, v_full, n_p, b, lp, heads, pad, mla_scale)

            q_lat = mla_absorb(q_full, wk2, sample_block, n_s, heads).reshape(n_s * heads, kv_lora)
            q_s = q_full[n_p:].reshape(n_s * heads, 256)
            ckv_new = jnp.pad(ckvb[n_p:].reshape(db, ds, kv_lora), ((0, 0), (0, ds_pad - ds), (0, 0)))
            kpe_new = jnp.pad(kpe[n_p:].reshape(db, ds, 128), ((0, 0), (0, ds_pad - ds), (0, 0)))
            o_lat = mla_decode(page_table, q_lat, q_s, ckv_new, kpe_new, cache_mla_ckv, pool_kpe_t,
                               l, heads, mla_scale)
            attn_s = mla_vup(o_lat.reshape(n_s, heads * kv_lora), wv2, heads)
            x, xb = proj_ln(attn_p, attn_s, w_o_mla[l].astype(BF16), x, g_mix, b_mix, alpha)
            ckv_out.append(ckv)
            kpe_out.append(kpe[:, :qk_rope])
        else:
            j = l - n_a
            if kv is None:
                w_kv = jnp.concatenate([w_k_shared, w_v_shared], -1).astype(BF16)
                kv = matmul_rows(xb, w_kv, F32)
                kvw = swa_kv_heads * swa_hd
                k_new = kv[n_p:, :kvw].reshape(db, ds, kvw)
                v_new = kv[n_p:, kvw:].reshape(db, ds, kvw)
                k_new_pad = jnp.pad(k_new, ((0, 0), (0, ds_pad - ds), (0, 0)))
                v_new_pad = jnp.pad(v_new, ((0, 0), (0, ds_pad - ds), (0, 0)))
            w_q = w_q_swa[j].reshape(d, swa_kv_heads // 2, 2, swa_group, swa_hd).transpose(0, 1, 3, 2, 4)
            w_o = w_o_swa[j].reshape(swa_kv_heads // 2, 2, swa_group, swa_hd, d).transpose(0, 2, 1, 3, 4)
            q = matmul_rows(xb, w_q.reshape(d, -1).astype(BF16), BF16)
            hs = jnp.stack([slopes, sinks[j].astype(F32)])
            attn_p = swa_attn_prompt(hs, q, kv, n_p, b, lp, pad, swa_group, swa_scale)
            q_s = jnp.pad(q[n_p:].reshape(db, ds, -1), ((0, 0), (0, ds_pad - ds), (0, 0)))
            hrow = jnp.broadcast_to(jnp.repeat(hs[:, head_of_block], ds_pad, axis=1)[:, :, None],
                                    (2, head_of_block.size * ds_pad, 128))
            attn_s = swa_attn_sample(hrow, q_s, cache_swa_k.reshape(db, win_buf, kvw),
                                     cache_swa_v.reshape(db, win_buf, kvw), k_new_pad, v_new_pad,
                                     past_len, swa_group, swa_scale)
            x, xb = proj_ln(attn_p, attn_s[:, :ds].reshape(n_s, -1), w_o.reshape(-1, d).astype(BF16),
                            x, g_mix, b_mix, alpha)

        g_ffn, b_ffn = ln_ffn_g[l][None], ln_ffn_b[l][None]
        f = l // 2
        if l % 2 == 0:
            h = ffn_gate_up(xb, w_gate, w_up, f)
            x, xb = ffn_down_ln(h, w_down, f, x, g_ffn, b_ffn, alpha)
        else:
            x, xb = moe_ffn_ln(x, w_router, we_gate, we_up, we_down, f, g_ffn, b_ffn, alpha)

    y_prompt = jnp.stack([x[i * lp + BLOCK:(i + 1) * lp] for i in range(b)])
    y_sample = x[n_p:].reshape(db, ds, d)
    new_ckv_prompt = jnp.stack([c[:n_p].reshape(b, lp, kv_lora)[:, pad:] for c in ckv_out])
    new_kpe_prompt = jnp.stack([c[:n_p].reshape(b, lp, qk_rope)[:, pad:] for c in kpe_out])
    new_ckv_sample = jnp.stack([c[n_p:].reshape(db, ds, kv_lora) for c in ckv_out])
    new_kpe_sample = jnp.stack([c[n_p:].reshape(db, ds, qk_rope) for c in kpe_out])
    wp = min(WINDOW, seq + n_meta)
    k_p = kv[:n_p, :kvw].reshape(b, lp, swa_kv_heads, swa_hd)
    v_p = kv[:n_p, kvw:].reshape(b, lp, swa_kv_heads, swa_hd)
    new_swa_k_prompt = k_p[:, lp - wp:]
    new_swa_v_prompt = v_p[:, lp - wp:]
    new_swa_k_sample = jnp.concatenate([cache_swa_k, k_new.reshape(db, ds, swa_kv_heads, swa_hd)], 1)[:, ds:]
    new_swa_v_sample = jnp.concatenate([cache_swa_v, v_new.reshape(db, ds, swa_kv_heads, swa_hd)], 1)[:, ds:]
    return (y_prompt, y_sample, new_ckv_prompt, new_kpe_prompt, new_ckv_sample, new_kpe_sample,
            new_swa_k_prompt, new_swa_v_prompt, new_swa_k_sample, new_swa_v_sample)
```

```python
import functools

import numpy as np
import jax
import jax.numpy as jnp
from jax import lax
from jax.experimental import pallas as pl
from jax.experimental.pallas import tpu as pltpu

F32 = jnp.float32
BF16 = jnp.bfloat16

BLOCK = 128
WINDOW = 128
ROPE_THETA = 10000.0
ALIBI_MAX_BIAS = 8.0
LN_EPS = 1e-5
RMS_EPS = 1e-6
NEG_INF = -1e30
LOG2E = 1.4426950408889634
N_EXPERT_LANES = 128
ROW_DMA_UNROLL = 8
MOE_ROW_TILE = 256
MOE_TILES_PER_GROUP = 10

V7X_VMEM_BYTES = 64 * 1024 * 1024
VMEM_CEILING = V7X_VMEM_BYTES - 6 * 1024 * 1024
MIB = 1024 * 1024


def _params(semantics, vmem_mib):
    return pltpu.CompilerParams(dimension_semantics=semantics,
                                vmem_limit_bytes=min(int(vmem_mib * MIB), VMEM_CEILING))


def _nt(a, b):
    return lax.dot_general(a, b, (((1,), (1,)), ((), ())), preferred_element_type=F32)


def _nn(a, b):
    return jnp.dot(a, b, preferred_element_type=F32)


def _rms(x, g):
    return x * lax.rsqrt(jnp.mean(jnp.square(x), -1, keepdims=True) + RMS_EPS) * g


def _ln_store(z, g_ref, b_ref, o_ref, ob_ref):
    mu = jnp.mean(z, -1, keepdims=True)
    d = z - mu
    var = jnp.mean(jnp.square(d), -1, keepdims=True)
    y = d * lax.rsqrt(var + LN_EPS) * g_ref[...] + b_ref[...]
    o_ref[...] = y
    ob_ref[...] = y.astype(ob_ref.dtype)


def _silu(g):
    return g * (1.0 / (1.0 + jnp.exp(-g)))


def _mm_kernel(x_ref, w_ref, o_ref):
    o_ref[...] = _nn(x_ref[...], w_ref[...]).astype(o_ref.dtype)


def matmul_rows(x, w, out_dtype, tm=512):
    m, k = x.shape
    n = w.shape[1]
    return pl.pallas_call(
        _mm_kernel,
        grid=(m // tm,),
        in_specs=[pl.BlockSpec((tm, k), lambda i: (i, 0)),
                  pl.BlockSpec((k, n), lambda i: (0, 0))],
        out_specs=pl.BlockSpec((tm, n), lambda i: (i, 0)),
        out_shape=jax.ShapeDtypeStruct((m, n), out_dtype),
        compiler_params=_params(("parallel",), 40),
        name="matmul_rows",
    )(x, w)


def _mla_down_kernel(x_ref, wq_ref, wkv_ref, wr_ref, gq_ref, gkv_ref, c_ref, s_ref,
                     cq_ref, ckv_ref, ckvb_ref, kpe_ref):
    x = x_ref[...].astype(wq_ref.dtype)
    cq_ref[...] = _rms(_nn(x, wq_ref[...]), gq_ref[...]).astype(cq_ref.dtype)
    ckv = _rms(_nn(x, wkv_ref[...]), gkv_ref[...])
    ckv_ref[...] = ckv
    ckvb_ref[...] = ckv.astype(ckvb_ref.dtype)
    r = _nn(x, wr_ref[...])
    half = r.shape[1] // 2
    kpe_ref[...] = r[:, :half] * c_ref[...] + r[:, half:] * s_ref[...]


def mla_down(xb, wq, wkv, wr, gq, gkv, ctab, stab, tm=512):
    n, d = xb.shape
    ql, kl = wq.shape[1], wkv.shape[1]
    row = lambda w: pl.BlockSpec((tm, w), lambda i: (i, 0))
    full = lambda a: pl.BlockSpec(a.shape, lambda i: (0,) * a.ndim)
    return pl.pallas_call(
        _mla_down_kernel,
        grid=(n // tm,),
        in_specs=[row(d), full(wq), full(wkv), full(wr), full(gq), full(gkv), row(128), row(128)],
        out_specs=[row(ql), row(kl), row(kl), row(128)],
        out_shape=[jax.ShapeDtypeStruct((n, ql), BF16), jax.ShapeDtypeStruct((n, kl), F32),
                   jax.ShapeDtypeStruct((n, kl), BF16), jax.ShapeDtypeStruct((n, 128), F32)],
        compiler_params=_params(("parallel",), 40),
        name="mla_down",
    )(xb, wq, wkv, wr, gq, gkv, ctab, stab)


def _mla_qup_kernel(cq_ref, w_ref, c_ref, s_ref, o_ref, *, heads_per_step):
    cq = cq_ref[...]
    c = c_ref[...]
    s = s_ref[...]
    for h in range(heads_per_step):
        acc = _nn(cq, w_ref[:, h * 384:(h + 1) * 384])
        o_ref[:, h * 256:h * 256 + 128] = acc[:, :128].astype(o_ref.dtype)
        o_ref[:, h * 256 + 128:(h + 1) * 256] = (acc[:, 128:256] * c + acc[:, 256:384] * s).astype(o_ref.dtype)


def mla_qup(cq, w, ctab, stab, heads, tm=512, hps=16):
    n, ql = cq.shape
    return pl.pallas_call(
        functools.partial(_mla_qup_kernel, heads_per_step=hps),
        grid=(n // tm, heads // hps),
        in_specs=[pl.BlockSpec((tm, ql), lambda i, j: (i, 0)),
                  pl.BlockSpec((ql, hps * 384), lambda i, j: (0, j)),
                  pl.BlockSpec((tm, 128), lambda i, j: (i, 0)),
                  pl.BlockSpec((tm, 128), lambda i, j: (i, 0))],
        out_specs=pl.BlockSpec((tm, hps * 256), lambda i, j: (i, j)),
        out_shape=jax.ShapeDtypeStruct((n, heads * 256), BF16),
        compiler_params=_params(("parallel", "parallel"), 32),
        name="mla_qup",
    )(cq, w, ctab, stab)


def _mla_kvup_kernel(c_ref, kpe_ref, wk_ref, wv_ref, k_ref, v_ref, *, heads_per_step):
    c = c_ref[...]
    kn = _nn(c, wk_ref[...])
    pe = kpe_ref[...].astype(k_ref.dtype)
    for h in range(heads_per_step):
        k_ref[:, h * 256:h * 256 + 128] = kn[:, h * 128:(h + 1) * 128].astype(k_ref.dtype)
        k_ref[:, h * 256 + 128:(h + 1) * 256] = pe
    v_ref[...] = _nn(c, wv_ref[...]).astype(v_ref.dtype)


def mla_kvup(ckvb, kpe, wk, wv, rows, heads, tm=512, hps=16):
    kl = ckvb.shape[1]
    return pl.pallas_call(
        functools.partial(_mla_kvup_kernel, heads_per_step=hps),
        grid=(rows // tm, heads // hps),
        in_specs=[pl.BlockSpec((tm, kl), lambda i, j: (i, 0)),
                  pl.BlockSpec((tm, 128), lambda i, j: (i, 0)),
                  pl.BlockSpec((kl, hps * 128), lambda i, j: (0, j)),
                  pl.BlockSpec((kl, hps * 128), lambda i, j: (0, j))],
        out_specs=[pl.BlockSpec((tm, hps * 256), lambda i, j: (i, j)),
                   pl.BlockSpec((tm, hps * 128), lambda i, j: (i, j))],
        out_shape=[jax.ShapeDtypeStruct((rows, heads * 256), BF16),
                   jax.ShapeDtypeStruct((rows, heads * 128), BF16)],
        compiler_params=_params(("parallel", "parallel"), 32),
        name="mla_kvup",
    )(ckvb, kpe, wk, wv)


def _mla_prompt_attn_kernel(q_ref, k_ref, v_ref, o_ref, *, blocks, pad, scale):
    scale2 = scale * LOG2E
    for r0, r1 in blocks:
        q = q_ref[r0:r1, :]
        pieces = [(0, min(BLOCK, r0)), (BLOCK, r0), (r0, r1)]
        scores = []
        for c0, c1 in pieces:
            if c1 <= c0:
                continue
            s = _nt(q, k_ref[c0:c1, :]) * scale2
            if c0 < pad:
                col = lax.broadcasted_iota(jnp.int32, s.shape, 1) + c0
                s = jnp.where(col >= pad, s, NEG_INF)
            if c1 > r0:
                row = lax.broadcasted_iota(jnp.int32, s.shape, 0) + r0
                col = lax.broadcasted_iota(jnp.int32, s.shape, 1) + c0
                s = jnp.where(col <= row, s, NEG_INF)
            scores.append((c0, c1, s))
        m = functools.reduce(jnp.maximum, [jnp.max(s, -1, keepdims=True) for _, _, s in scores])
        l = 0.0
        o = 0.0
        for c0, c1, s in scores:
            p = jnp.exp2(s - m)
            l = l + jnp.sum(p, -1, keepdims=True)
            o = o + _nn(p.astype(v_ref.dtype), v_ref[c0:c1, :])
        o_ref[r0:r1, :] = (o / l).astype(o_ref.dtype)


def mla_prompt_attn(q, k, v, n_rows, batch, lp, heads, pad, scale):
    blocks = [(0, BLOCK)] + [(r, r + 256) for r in range(BLOCK, lp, 256)]
    assert blocks[-1][1] == lp
    return pl.pallas_call(
        functools.partial(_mla_prompt_attn_kernel, blocks=tuple(blocks), pad=pad, scale=scale),
        grid=(batch, heads),
        in_specs=[pl.BlockSpec((lp, 256), lambda b, h: (b, h)),
                  pl.BlockSpec((lp, 256), lambda b, h: (b, h)),
                  pl.BlockSpec((lp, 128), lambda b, h: (b, h))],
        out_specs=pl.BlockSpec((lp, 128), lambda b, h: (b, h)),
        out_shape=jax.ShapeDtypeStruct((n_rows, heads * 128), BF16),
        compiler_params=_params(("parallel", "parallel"), 40),
        name="mla_prompt_attn",
    )(q, k, v)


def _mla_absorb_kernel(q_ref, w_ref, o_ref):
    o_ref[...] = _nt(q_ref[:, :128], w_ref[...]).astype(o_ref.dtype)


def mla_absorb(q_full, wk, row_block, rows, heads):
    kl = wk.shape[0]
    return pl.pallas_call(
        _mla_absorb_kernel,
        grid=(heads,),
        in_specs=[pl.BlockSpec((rows, 256), lambda h: (row_block, h)),
                  pl.BlockSpec((kl, 128), lambda h: (0, h))],
        out_specs=pl.BlockSpec((rows, kl), lambda h: (0, h)),
        out_shape=jax.ShapeDtypeStruct((rows, heads * kl), BF16),
        compiler_params=_params(("parallel",), 32),
        name="mla_absorb",
    )(q_full, wk)


def _mla_decode_kernel(pt_ref, ql_ref, q_ref, cn_ref, kn_ref, ckv_hbm, kpe_hbm, o_ref,
                       ckv_buf, kpe_buf, sem, m_sc, l_sc, acc_sc, *, layer, pages, scale, heads, rope):
    j = pl.program_id(1)
    n_chunks = pl.num_programs(1)
    step = pl.program_id(0) * n_chunks + j
    page = ckv_hbm.shape[2]

    def page_copies(page_id, slot, i):
        return (pltpu.make_async_copy(ckv_hbm.at[layer, page_id], ckv_buf.at[slot, pl.ds(i * page, page), :],
                                      sem.at[0, slot]),
                pltpu.make_async_copy(kpe_hbm.at[layer, page_id], kpe_buf.at[slot, i], sem.at[1, slot]))

    def start_chunk(chunk, slot):
        for i in range(pages):
            for copy in page_copies(pt_ref[chunk * pages + i], slot, i):
                copy.start()

    @pl.when(step == 0)
    def _():
        start_chunk(0, 0)

    @pl.when(step + 1 < pl.num_programs(0) * n_chunks)
    def _():
        start_chunk(step + 1, (step + 1) % 2)

    slot = step % 2
    for i in range(pages):
        for copy in page_copies(0, slot, i):
            copy.wait()

    @pl.when(j == 0)
    def _():
        m_sc[...] = jnp.full(m_sc.shape, NEG_INF, F32)
        l_sc[...] = jnp.zeros(l_sc.shape, F32)
        acc_sc[...] = jnp.zeros(acc_sc.shape, F32)

    ql = ql_ref[...]
    qp = q_ref[:, 128:128 + rope]
    scale2 = scale * LOG2E

    def update(s, vals):
        m_prev = m_sc[...]
        m_new = jnp.maximum(m_prev, jnp.max(s, -1, keepdims=True))
        a = jnp.exp2(m_prev - m_new)
        p = jnp.exp2(s - m_new)
        l_sc[...] = a * l_sc[...] + jnp.sum(p, -1, keepdims=True)
        acc_sc[...] = a * acc_sc[...] + _nn(p.astype(vals.dtype), vals)
        m_sc[...] = m_new

    ck = ckv_buf[slot].astype(BF16)
    kpt = jnp.concatenate([kpe_buf[slot, i].astype(BF16) for i in range(pages)], axis=1)
    update((_nt(ql, ck) + _nn(qp, kpt)) * scale2, ck)

    @pl.when(j == pl.num_programs(1) - 1)
    def _():
        cn = cn_ref[...]
        kn = kn_ref[:, :rope].astype(BF16)
        s = (_nt(ql, cn) + _nt(qp, kn)) * scale2
        tok = lax.broadcasted_iota(jnp.int32, s.shape, 0) // heads
        col = lax.broadcasted_iota(jnp.int32, s.shape, 1)
        update(jnp.where(col <= tok, s, NEG_INF), cn)
        o_ref[...] = (acc_sc[...] / l_sc[...]).astype(o_ref.dtype)


def mla_decode(page_table, q_lat, q_s, ckv_new, kpe_new, pool_ckv, pool_kpe_t, layer, heads, scale, pages=32):
    db, n_pages = page_table.shape
    page, kl = pool_ckv.shape[2], pool_ckv.shape[3]
    rope = pool_kpe_t.shape[2]
    rows = q_lat.shape[0] // db
    new_rows = ckv_new.shape[1]
    pages = min(pages, n_pages)
    assert n_pages % pages == 0
    pt = page_table.reshape(-1)
    hbm = pl.BlockSpec(memory_space=pl.ANY)
    in_specs = [pl.BlockSpec((rows, kl), lambda b, j, pt_ref: (b, 0)),
                pl.BlockSpec((rows, 256), lambda b, j, pt_ref: (b, 0)),
                pl.BlockSpec((None, new_rows, kl), lambda b, j, pt_ref: (b, 0, 0)),
                pl.BlockSpec((None, new_rows, 128), lambda b, j, pt_ref: (b, 0, 0)),
                hbm, hbm]
    return pl.pallas_call(
        functools.partial(_mla_decode_kernel, layer=layer, pages=pages, scale=scale, heads=heads, rope=rope),
        grid_spec=pltpu.PrefetchScalarGridSpec(
            num_scalar_prefetch=1,
            grid=(db, n_pages // pages),
            in_specs=in_specs,
            out_specs=pl.BlockSpec((rows, kl), lambda b, j, pt_ref: (b, 0)),
            scratch_shapes=[pltpu.VMEM((2, pages * page, kl), F32), pltpu.VMEM((2, pages, rope, page), F32),
                            pltpu.SemaphoreType.DMA((2, 2)),
                            pltpu.VMEM((rows, 1), F32), pltpu.VMEM((rows, 1), F32),
                            pltpu.VMEM((rows, kl), F32)]),
        out_shape=jax.ShapeDtypeStruct(q_lat.shape, BF16),
        compiler_params=_params(("arbitrary", "arbitrary"), 48),
        name="mla_decode",
    )(pt, q_lat, q_s, ckv_new, kpe_new, pool_ckv, pool_kpe_t)


def mla_vup(o_lat2d, wv, heads):
    rows = o_lat2d.shape[0]
    kl = wv.shape[0]
    return pl.pallas_call(
        _mm_kernel,
        grid=(heads,),
        in_specs=[pl.BlockSpec((rows, kl), lambda h: (0, h)),
                  pl.BlockSpec((kl, 128), lambda h: (0, h))],
        out_specs=pl.BlockSpec((rows, 128), lambda h: (0, h)),
        out_shape=jax.ShapeDtypeStruct((rows, heads * 128), BF16),
        compiler_params=_params(("parallel",), 32),
        name="mla_vup",
    )(o_lat2d, wv)


def _proj_ln_kernel(xp_ref, xs_ref, w_ref, r_ref, g_ref, b_ref, o_ref, ob_ref, *, alpha, prompt_blocks):
    x = jnp.where(pl.program_id(0) < prompt_blocks, xp_ref[...], xs_ref[...])
    z = alpha * r_ref[...] + _nn(x, w_ref[...])
    _ln_store(z, g_ref, b_ref, o_ref, ob_ref)


def proj_ln(x_p, x_s, w, resid, g, b, alpha, tm=512):
    k = x_p.shape[1]
    n, d = resid.shape
    pb = x_p.shape[0] // tm
    row = lambda wd: pl.BlockSpec((tm, wd), lambda i: (i, 0))
    full = lambda a: pl.BlockSpec(a.shape, lambda i: (0,) * a.ndim)
    return pl.pallas_call(
        functools.partial(_proj_ln_kernel, alpha=alpha, prompt_blocks=pb),
        grid=(n // tm,),
        in_specs=[pl.BlockSpec((tm, k), lambda i: (jnp.minimum(i, pb - 1), 0)),
                  pl.BlockSpec((tm, k), lambda i: (jnp.maximum(i - pb, 0), 0)),
                  full(w), row(d), full(g), full(b)],
        out_specs=[row(d), row(d)],
        out_shape=[jax.ShapeDtypeStruct((n, d), F32), jax.ShapeDtypeStruct((n, d), BF16)],
        compiler_params=_params(("parallel",), 48),
        name="proj_ln",
    )(x_p, x_s, w, resid, g, b)


def _ffn_gu_kernel(x_ref, wg_ref, wu_ref, o_ref, wgb, wub):
    @pl.when(pl.program_id(1) == 0)
    def _():
        wgb[...] = wg_ref[...].astype(BF16)
        wub[...] = wu_ref[...].astype(BF16)

    x = x_ref[...]
    g = _nn(x, wgb[...])
    u = _nn(x, wub[...])
    o_ref[...] = (_silu(g) * u).astype(o_ref.dtype)


def ffn_gate_up(xb, w_gate, w_up, f, tm=1024, tn=512):
    n, d = xb.shape
    dff = w_gate.shape[2]
    wspec = pl.BlockSpec((None, d, tn), lambda j, i: (f, 0, j))
    return pl.pallas_call(
        _ffn_gu_kernel,
        grid=(dff // tn, n // tm),
        in_specs=[pl.BlockSpec((tm, d), lambda j, i: (i, 0)), wspec, wspec],
        out_specs=pl.BlockSpec((tm, tn), lambda j, i: (i, j)),
        out_shape=jax.ShapeDtypeStruct((n, dff), BF16),
        scratch_shapes=[pltpu.VMEM((d, tn), BF16), pltpu.VMEM((d, tn), BF16)],
        compiler_params=_params(("arbitrary", "arbitrary"), 48),
        name="ffn_gate_up",
    )(xb, w_gate, w_up)


def _ffn_down_ln_kernel(h_ref, w_ref, r_ref, g_ref, b_ref, o_ref, ob_ref, acc, *, alpha):
    k = pl.program_id(1)

    @pl.when(k == 0)
    def _():
        acc[...] = jnp.zeros(acc.shape, F32)

    acc[...] += _nn(h_ref[...], w_ref[...].astype(BF16))

    @pl.when(k == pl.num_programs(1) - 1)
    def _():
        _ln_store(alpha * r_ref[...] + acc[...], g_ref, b_ref, o_ref, ob_ref)


def ffn_down_ln(h, w_down, f, resid, g, b, alpha, tm=1024, tk=512):
    n, dff = h.shape
    d = w_down.shape[2]
    row = pl.BlockSpec((tm, d), lambda i, k: (i, 0), pipeline_mode=pl.Buffered(1))
    vec = lambda a: pl.BlockSpec(a.shape, lambda i, k: (0,) * a.ndim)
    return pl.pallas_call(
        functools.partial(_ffn_down_ln_kernel, alpha=alpha),
        grid=(n // tm, dff // tk),
        in_specs=[pl.BlockSpec((tm, tk), lambda i, k: (i, k)),
                  pl.BlockSpec((None, tk, d), lambda i, k: (f, k, 0)),
                  row, vec(g), vec(b)],
        out_specs=[row, row],
        out_shape=[jax.ShapeDtypeStruct((n, d), F32), jax.ShapeDtypeStruct((n, d), BF16)],
        scratch_shapes=[pltpu.VMEM((tm, d), F32)],
        compiler_params=_params(("parallel", "arbitrary"), 56),
        name="ffn_down_ln",
    )(h, w_down, resid, g, b)


def _router_kernel(x_ref, w_ref, info_ref, cnt_ref, x3_ref, carry, *, n_experts):
    i = pl.program_id(0)

    @pl.when(i == 0)
    def _():
        carry[...] = jnp.zeros(carry.shape, F32)

    logits = jnp.dot(x_ref[...], w_ref[...], preferred_element_type=F32, precision=lax.Precision.HIGHEST)
    tm = logits.shape[0]
    lane = lax.broadcasted_iota(jnp.int32, logits.shape, 1).astype(F32)
    lowest = float(np.finfo(np.float32).min)
    lg = jnp.where(lane < n_experts, logits, lowest)
    v1 = jnp.max(lg, -1, keepdims=True)
    i1 = jnp.min(jnp.where(lg == v1, lane, float(N_EXPERT_LANES)), -1, keepdims=True)
    lg2 = jnp.where(lane == i1, lowest, lg)
    v2 = jnp.max(lg2, -1, keepdims=True)
    i2 = jnp.min(jnp.where(lg2 == v2, lane, float(N_EXPERT_LANES)), -1, keepdims=True)
    e2 = jnp.exp(v2 - v1)
    w1 = 1.0 / (1.0 + e2)
    w2 = e2 / (1.0 + e2)
    sel1 = lane == i1
    sel2 = lane == i2
    onehot = jnp.where(sel1, 1.0, jnp.where(sel2, 1.0, 0.0))
    r = lax.broadcasted_iota(jnp.int32, (tm, tm), 0)
    c = lax.broadcasted_iota(jnp.int32, (tm, tm), 1)
    lower = jnp.where(c < r, 1.0, 0.0).astype(BF16)
    before = _nn(lower, onehot.astype(BF16)) + carry[...]
    rank1 = jnp.sum(jnp.where(sel1, before, 0.0), -1, keepdims=True)
    rank2 = jnp.sum(jnp.where(sel2, before, 0.0), -1, keepdims=True)
    total = carry[...] + jnp.sum(onehot, 0, keepdims=True)
    carry[...] = total
    cols = (i1, i2, rank1, rank2, w1, w2)
    info = jnp.zeros(logits.shape, F32)
    for k, val in enumerate(cols):
        info = jnp.where(lane == k, val, info)
    info_ref[...] = info
    cnt_ref[...] = jnp.broadcast_to(total, cnt_ref.shape)
    chunks = x_ref.shape[1] // 128
    for j in range(chunks):
        x3_ref[pl.ds(j, tm, stride=chunks), :] = x_ref[:, j * 128:(j + 1) * 128]


def moe_router(x, w_pad, n_experts, tm=512):
    n, d = x.shape
    chunks = d // 128
    return pl.pallas_call(
        functools.partial(_router_kernel, n_experts=n_experts),
        grid=(n // tm,),
        in_specs=[pl.BlockSpec((tm, d), lambda i: (i, 0)),
                  pl.BlockSpec(w_pad.shape, lambda i: (0, 0))],
        out_specs=[pl.BlockSpec((tm, N_EXPERT_LANES), lambda i: (i, 0)),
                   pl.BlockSpec((8, N_EXPERT_LANES), lambda i: (0, 0)),
                   pl.BlockSpec((tm * chunks, 128), lambda i: (i, 0))],
        out_shape=[jax.ShapeDtypeStruct((n, N_EXPERT_LANES), F32),
                   jax.ShapeDtypeStruct((8, N_EXPERT_LANES), F32),
                   jax.ShapeDtypeStruct((n * chunks, 128), F32)],
        scratch_shapes=[pltpu.VMEM((1, N_EXPERT_LANES), F32)],
        compiler_params=_params(("arbitrary",), 40),
        name="moe_router",
    )(x, w_pad)


def _moe_gather_kernel(tok_ref, x3_hbm, o_ref, buf, sem):
    tm = o_ref.shape[0]
    chunks = x3_hbm.shape[1]
    i = pl.program_id(0)

    def row_copy(token, slot, r):
        return pltpu.make_async_copy(x3_hbm.at[token], buf.at[slot, pl.ds(r * chunks, chunks), :], sem.at[slot])

    def start_tile(tile, slot):
        def body(g, carry):
            for u in range(ROW_DMA_UNROLL):
                r = g * ROW_DMA_UNROLL + u
                row_copy(tok_ref[tile * tm + r], slot, r).start(priority=u % 2)
            return carry
        lax.fori_loop(0, tm // ROW_DMA_UNROLL, body, 0)

    @pl.when(i == 0)
    def _():
        start_tile(0, 0)

    @pl.when(i + 1 < pl.num_programs(0))
    def _():
        start_tile(i + 1, (i + 1) % 2)

    slot = i % 2

    def wait(r, carry):
        row_copy(0, slot, r).wait()
        return carry

    lax.fori_loop(0, tm, wait, 0, unroll=8)
    for j in range(chunks):
        o_ref[:, j * 128:(j + 1) * 128] = buf[slot, pl.ds(j, tm, stride=chunks), :].astype(o_ref.dtype)


def moe_gather(token_of_row, x3, tm=256):
    rows = token_of_row.shape[0]
    chunks = x3.shape[1]
    return pl.pallas_call(
        _moe_gather_kernel,
        grid_spec=pltpu.PrefetchScalarGridSpec(
            num_scalar_prefetch=1,
            grid=(rows // tm,),
            in_specs=[pl.BlockSpec(memory_space=pl.ANY)],
            out_specs=pl.BlockSpec((tm, chunks * 128), lambda i, tok: (i, 0)),
            scratch_shapes=[pltpu.VMEM((2, tm * chunks, 128), F32), pltpu.SemaphoreType.DMA((2,))]),
        out_shape=jax.ShapeDtypeStruct((rows, chunks * 128), BF16),
        compiler_params=_params(("arbitrary",), 32),
        name="moe_gather",
    )(token_of_row, x3)


def _moe_experts_kernel(se_ref, st_ref, sn_ref, sz_ref, xs_hbm, wg_hbm, wu_hbm, wd_hbm, y_hbm,
                        x_buf, acc, wg_buf, wu_buf, wd_buf, sem, *, layer, tile, tiles_per_group):
    s = pl.program_id(0)
    expert = se_ref[s]
    n_compute = sn_ref[s]
    n_zero = sz_ref[s]
    first_tile = st_ref[s]
    tf = wg_buf.shape[2]
    n_f = wg_hbm.shape[3] // tf
    row_sem, out_sem, wg_sem, wu_sem, wd_sem = range(5)

    def rows(t):
        return pl.ds(t * tile, tile)

    def tile_in(t):
        return pltpu.make_async_copy(xs_hbm.at[pl.ds((first_tile + t) * tile, tile), :], x_buf.at[rows(t), :],
                                     sem.at[row_sem, 0])

    def tile_out(t):
        return pltpu.make_async_copy(acc.at[rows(t), :], y_hbm.at[pl.ds((first_tile + t) * tile, tile), :],
                                     sem.at[out_sem, 0])

    def weight_copies(f, slot):
        block = pl.ds(pl.multiple_of(f * tf, tf), tf)
        return (pltpu.make_async_copy(wg_hbm.at[layer, expert, :, block], wg_buf.at[slot], sem.at[wg_sem, slot]),
                pltpu.make_async_copy(wu_hbm.at[layer, expert, :, block], wu_buf.at[slot], sem.at[wu_sem, slot]),
                pltpu.make_async_copy(wd_hbm.at[layer, expert, block, :], wd_buf.at[slot], sem.at[wd_sem, slot]))

    def for_tiles(count, fn):
        for t in range(tiles_per_group):
            @pl.when(t < count)
            def _(t=t):
                fn(t)

    @pl.when(n_compute > 0)
    def _():
        for copy in weight_copies(0, 0):
            copy.start()
        for_tiles(n_compute, lambda t: tile_in(t).start())
        acc[...] = jnp.zeros(acc.shape, F32)
        for_tiles(n_compute, lambda t: tile_in(t).wait())

        def d_ff_block(f, carry):
            slot = f % 2

            @pl.when(f + 1 < n_f)
            def _():
                for copy in weight_copies(f + 1, 1 - slot):
                    copy.start()

            for copy in weight_copies(f, slot):
                copy.wait()
            w_gu = jnp.concatenate([wg_buf[slot].astype(BF16), wu_buf[slot].astype(BF16)], axis=1)
            w_d = wd_buf[slot].astype(BF16)

            def compute(block):
                gu = _nn(x_buf[block, :], w_gu)
                h = (_silu(gu[:, :tf]) * gu[:, tf:]).astype(BF16)
                acc[block, :] += _nn(h, w_d)

            for pair in range(tiles_per_group // 2):
                @pl.when(2 * pair + 2 <= n_compute)
                def _(pair=pair):
                    compute(pl.ds(2 * pair * tile, 2 * tile))

                @pl.when(2 * pair + 1 == n_compute)
                def _(pair=pair):
                    compute(pl.ds(2 * pair * tile, tile))
            return carry

        lax.fori_loop(0, n_f, d_ff_block, 0)
        for_tiles(n_compute, lambda t: tile_out(t).start())
        for_tiles(n_compute, lambda t: tile_out(t).wait())

    @pl.when(n_zero > 0)
    def _():
        acc[...] = jnp.zeros(acc.shape, F32)
        for_tiles(n_zero, lambda t: tile_out(t).start())
        for_tiles(n_zero, lambda t: tile_out(t).wait())


def moe_experts(sup_expert, sup_tile, sup_compute, sup_zero, xs, we_gate, we_up, we_down, layer,
                tile, tiles_per_group, tf=256):
    rows, d = xs.shape
    dff = we_gate.shape[3]
    n_groups = sup_expert.shape[0]
    assert dff % tf == 0 and tiles_per_group % 2 == 0
    hbm = pl.BlockSpec(memory_space=pl.ANY)
    return pl.pallas_call(
        functools.partial(_moe_experts_kernel, layer=layer, tile=tile, tiles_per_group=tiles_per_group),
        grid_spec=pltpu.PrefetchScalarGridSpec(
            num_scalar_prefetch=4,
            grid=(n_groups,),
            in_specs=[hbm, hbm, hbm, hbm],
            out_specs=hbm,
            scratch_shapes=[pltpu.VMEM((tiles_per_group * tile, d), BF16),
                            pltpu.VMEM((tiles_per_group * tile, d), F32),
                            pltpu.VMEM((2, d, tf), F32), pltpu.VMEM((2, d, tf), F32), pltpu.VMEM((2, tf, d), F32),
                            pltpu.SemaphoreType.DMA((5, 2))]),
        out_shape=jax.ShapeDtypeStruct((rows, d), F32),
        compiler_params=_params(("arbitrary",), 58),
        name="moe_experts",
    )(sup_expert, sup_tile, sup_compute, sup_zero, xs, we_gate, we_up, we_down)


def _moe_combine_ln_kernel(p1_ref, p2_ref, y_hbm, info_ref, r_ref, g_ref, b_ref, o_ref, ob_ref,
                           buf, sem, *, alpha):
    tm = o_ref.shape[0]
    i = pl.program_id(0)
    pos_refs = (p1_ref, p2_ref)

    def row_copy(src_row, slot, which, r):
        return pltpu.make_async_copy(y_hbm.at[pl.ds(src_row, 1), :], buf.at[slot, which, pl.ds(r, 1), :],
                                     sem.at[slot, which])

    def start_tile(tile, slot):
        def body(g, carry):
            for u in range(ROW_DMA_UNROLL):
                r = g * ROW_DMA_UNROLL + u
                for which in range(2):
                    row_copy(pos_refs[which][tile * tm + r], slot, which, r).start(priority=which)
            return carry
        lax.fori_loop(0, tm // ROW_DMA_UNROLL, body, 0)

    @pl.when(i == 0)
    def _():
        start_tile(0, 0)

    @pl.when(i + 1 < pl.num_programs(0))
    def _():
        start_tile(i + 1, (i + 1) % 2)

    slot = i % 2

    def wait(r, carry):
        for which in range(2):
            row_copy(0, slot, which, r).wait()
        return carry

    lax.fori_loop(0, tm, wait, 0, unroll=ROW_DMA_UNROLL)
    info = info_ref[...]
    f = info[:, 4:5] * buf[slot, 0] + info[:, 5:6] * buf[slot, 1]
    _ln_store(alpha * r_ref[...] + f, g_ref, b_ref, o_ref, ob_ref)


def moe_combine_ln(pos1, pos2, y, info, resid, g, b, alpha, tm=256):
    n, d = resid.shape
    row = lambda w: pl.BlockSpec((tm, w), lambda i, p1, p2: (i, 0))
    vec = lambda a: pl.BlockSpec(a.shape, lambda i, p1, p2: (0,) * a.ndim)
    return pl.pallas_call(
        functools.partial(_moe_combine_ln_kernel, alpha=alpha),
        grid_spec=pltpu.PrefetchScalarGridSpec(
            num_scalar_prefetch=2,
            grid=(n // tm,),
            in_specs=[pl.BlockSpec(memory_space=pl.ANY), row(N_EXPERT_LANES), row(d), vec(g), vec(b)],
            out_specs=[row(d), row(d)],
            scratch_shapes=[pltpu.VMEM((2, 2, tm, d), F32), pltpu.SemaphoreType.DMA((2, 2))]),
        out_shape=[jax.ShapeDtypeStruct((n, d), F32), jax.ShapeDtypeStruct((n, d), BF16)],
        compiler_params=_params(("arbitrary",), 40),
        name="moe_combine_ln",
    )(pos1, pos2, y, info, resid, g, b)


def moe_ffn_ln(x, w_router, we_gate, we_up, we_down, f, g, b, alpha, tile=MOE_ROW_TILE):
    n, d = x.shape
    n_experts = w_router.shape[2]
    w_pad = jnp.pad(w_router[f], ((0, 0), (0, N_EXPERT_LANES - n_experts)))
    info, cnt, x3 = moe_router(x, w_pad, n_experts)
    e1 = info[:, 0].astype(jnp.int32)
    e2 = info[:, 1].astype(jnp.int32)
    counts = cnt[0, :n_experts].astype(jnp.int32)
    group = (counts + tile - 1) // tile * tile
    ends = jnp.cumsum(group)
    starts = ends - group
    pos1 = starts[e1] + info[:, 2].astype(jnp.int32)
    pos2 = starts[e2] + info[:, 3].astype(jnp.int32)
    rows = 2 * n + n_experts * tile
    tok = jnp.arange(n, dtype=jnp.int32)
    token_of_row = jnp.zeros((rows,), jnp.int32).at[jnp.concatenate([pos1, pos2])].set(
        jnp.concatenate([tok, tok]), unique_indices=True)
    tpg = MOE_TILES_PER_GROUP
    n_tiles = rows // tile
    n_valid = ends[-1] // tile
    tiles_e = group // tile
    groups_e = (tiles_e + tpg - 1) // tpg
    g_end = jnp.cumsum(groups_e)
    g_start = g_end - groups_e
    n_compute = g_end[-1]
    n_groups = (n_tiles + (tpg - 1) * (n_experts + 1)) // tpg + 1
    sidx = jnp.arange(n_groups, dtype=jnp.int32)
    e_of = jnp.minimum(jnp.sum((sidx[:, None] >= g_end[None, :]).astype(jnp.int32), -1), n_experts - 1)
    k = sidx - g_start[e_of]
    is_compute = sidx < n_compute
    sup_compute = jnp.where(is_compute, jnp.clip(tiles_e[e_of] - k * tpg, 0, tpg), 0)
    zero_tile = n_valid + (sidx - n_compute) * tpg
    sup_zero = jnp.where(is_compute, 0, jnp.clip(n_tiles - zero_tile, 0, tpg))
    sup_tile = jnp.where(is_compute, starts[e_of] // tile + k * tpg, jnp.where(sup_zero > 0, zero_tile, 0))
    sup_expert = jnp.where(is_compute, e_of, e_of[n_compute - 1])
    as_i32 = lambda a: a.astype(jnp.int32)

    xs = moe_gather(token_of_row, x3.reshape(n, d // 128, 128))
    y = moe_experts(as_i32(sup_expert), as_i32(sup_tile), as_i32(sup_compute), as_i32(sup_zero), xs,
                    we_gate, we_up, we_down, f, tile, tpg)
    return moe_combine_ln(pos1, pos2, y, info, x, g, b, alpha)


def _swa_kernel(hs_ref, q_ref, kp_ref, kc_ref, vp_ref, vc_ref, o_ref, *, tq, group, scale, start_fn):
    q_start = start_fn(pl.program_id(0), pl.program_id(1))
    n_pairs = kp_ref.shape[1] // 128
    lane = lax.broadcasted_iota(jnp.int32, (1, 128), 1)
    low = lane < 64
    rows = group * tq
    r = lax.broadcasted_iota(jnp.int32, (rows, 1), 0) % tq
    c_prev = lax.broadcasted_iota(jnp.int32, (1, kp_ref.shape[0]), 1)
    c_cur = lax.broadcasted_iota(jnp.int32, (1, tq), 1)
    dist_prev = r + WINDOW - c_prev
    dist_cur = r - c_cur
    in_window = lambda dist: jnp.logical_and(dist >= 0, dist < WINDOW)
    ok_prev = jnp.logical_and(in_window(dist_prev), (q_start - WINDOW + c_prev) >= 0)
    ok_cur = jnp.logical_and(in_window(dist_cur), (q_start + c_cur) >= 0)
    dprev = dist_prev.astype(F32)
    dcur = dist_cur.astype(F32)
    blk = lax.broadcasted_iota(jnp.int32, (rows, 1), 0) // tq

    for p in range(n_pairs):
        sl = slice(p * 128, (p + 1) * 128)
        kp = kp_ref[:, sl].astype(BF16)
        kc = kc_ref[:, sl].astype(BF16)
        vp = vp_ref[:, sl].astype(BF16)
        vc = vc_ref[:, sl].astype(BF16)
        q4 = jnp.concatenate([q_ref[:, (p * group + i) * 128:(p * group + i + 1) * 128]
                              for i in range(group)], axis=0)
        out = None
        for half in range(2):
            keep = low if half == 0 else jnp.logical_not(low)
            zero = jnp.zeros((), BF16)
            slope = jnp.zeros((rows, 1), F32)
            sink = jnp.zeros((rows, 1), F32)
            for i in range(group):
                head = (2 * p + half) * group + i
                slope = jnp.where(blk == i, hs_ref[0, head], slope)
                sink = jnp.where(blk == i, hs_ref[1, head], sink)
            s_prev = _nt(q4, jnp.where(keep, kp, zero)) * scale - slope * dprev
            s_cur = _nt(q4, jnp.where(keep, kc, zero)) * scale - slope * dcur
            s_prev = jnp.where(ok_prev, s_prev, NEG_INF)
            s_cur = jnp.where(ok_cur, s_cur, NEG_INF)
            m = jnp.maximum(jnp.maximum(jnp.max(s_prev, -1, keepdims=True),
                                        jnp.max(s_cur, -1, keepdims=True)), sink)
            e_prev = jnp.exp(s_prev - m)
            e_cur = jnp.exp(s_cur - m)
            denom = jnp.sum(e_prev, -1, keepdims=True) + jnp.sum(e_cur, -1, keepdims=True) + jnp.exp(sink - m)
            o = (_nn((e_prev / denom).astype(BF16), jnp.where(keep, vp, zero))
                 + _nn((e_cur / denom).astype(BF16), jnp.where(keep, vc, zero)))
            out = o if out is None else out + o
        for i in range(group):
            o_ref[:, (p * group + i) * 128:(p * group + i + 1) * 128] = out[i * tq:(i + 1) * tq].astype(o_ref.dtype)


def swa_attn_prompt(hs, q, kv, attn_rows, batch, lp, pad, group, scale):
    nb = lp // BLOCK
    kvw = kv.shape[1] // 2
    d = q.shape[1]
    smem = pl.BlockSpec(memory_space=pltpu.SMEM)
    return pl.pallas_call(
        functools.partial(_swa_kernel, tq=BLOCK, group=group, scale=scale,
                          start_fn=lambda b, j: j * BLOCK - pad),
        grid=(batch, nb),
        in_specs=[smem,
                  pl.BlockSpec((BLOCK, d), lambda b, j: (b * nb + j, 0)),
                  pl.BlockSpec((BLOCK, kvw), lambda b, j: (b * nb + jnp.maximum(j - 1, 0), 0)),
                  pl.BlockSpec((BLOCK, kvw), lambda b, j: (b * nb + j, 0)),
                  pl.BlockSpec((BLOCK, kvw), lambda b, j: (b * nb + jnp.maximum(j - 1, 0), 1)),
                  pl.BlockSpec((BLOCK, kvw), lambda b, j: (b * nb + j, 1))],
        out_specs=pl.BlockSpec((BLOCK, d), lambda b, j: (b * nb + j, 0)),
        out_shape=jax.ShapeDtypeStruct((attn_rows, d), BF16),
        compiler_params=_params(("parallel", "parallel"), 32),
        name="swa_attn_prompt",
    )(hs, q, kv, kv, kv, kv)


def _swa_sample_kernel(hrow_ref, q_ref, kp_ref, kc_ref, vp_ref, vc_ref, o_ref, *, group, scale, past_len):
    seqs, tq, d = q_ref.shape
    n_slabs = d // 128
    n_pairs = kp_ref.shape[2] // 128
    rows = 2 * n_slabs * tq
    lane = lax.broadcasted_iota(jnp.int32, (1, 128), 1)
    low = lane < 64
    t = lax.broadcasted_iota(jnp.int32, (rows, 1), 0) % tq
    c_prev = lax.broadcasted_iota(jnp.int32, (1, WINDOW), 1)
    c_cur = lax.broadcasted_iota(jnp.int32, (1, tq), 1)
    dist_prev = t + WINDOW - c_prev
    dist_cur = t - c_cur
    in_window = lambda dist: jnp.logical_and(dist >= 0, dist < WINDOW)
    ok_prev = jnp.logical_and(in_window(dist_prev), (past_len - WINDOW + c_prev) >= 0)
    ok_cur = in_window(dist_cur)
    slope = hrow_ref[0]
    sink = hrow_ref[1][:, :1]
    bias_prev = slope * dist_prev.astype(F32)
    bias_cur = slope[:, :tq] * dist_cur.astype(F32)
    zero_slab = jnp.zeros((tq, 128), BF16)
    zero = jnp.zeros((), BF16)

    for s in range(seqs):
        blocks = []
        for sidx in range(n_slabs):
            p = sidx // group
            slab = q_ref[s, :, sidx * 128:(sidx + 1) * 128]
            for half in range(2):
                own = jnp.where(low, slab, zero) if half == 0 else jnp.where(low, zero, slab)
                blocks.append(jnp.concatenate([zero_slab] * p + [own] + [zero_slab] * (n_pairs - 1 - p), axis=1))
        qexp = jnp.concatenate(blocks, axis=0)
        kp = kp_ref[s].astype(BF16)
        kc = kc_ref[s].astype(BF16)
        s_prev = jnp.where(ok_prev, _nt(qexp, kp) * scale - bias_prev, NEG_INF)
        s_cur = jnp.where(ok_cur, _nt(qexp, kc) * scale - bias_cur, NEG_INF)
        m = jnp.maximum(jnp.maximum(jnp.max(s_prev, -1, keepdims=True), jnp.max(s_cur, -1, keepdims=True)), sink)
        e_prev = jnp.exp(s_prev - m)
        e_cur = jnp.exp(s_cur - m)
        denom = jnp.sum(e_prev, -1, keepdims=True) + jnp.sum(e_cur, -1, keepdims=True) + jnp.exp(sink - m)
        o = (_nn((e_prev / denom).astype(BF16), vp_ref[s].astype(BF16))
             + _nn((e_cur / denom).astype(BF16), vc_ref[s].astype(BF16)))
        for sidx in range(n_slabs):
            p = sidx // group
            r0 = 2 * sidx * tq
            lo = o[r0:r0 + tq, p * 128:(p + 1) * 128]
            hi = o[r0 + tq:r0 + 2 * tq, p * 128:(p + 1) * 128]
            o_ref[s, :, sidx * 128:(sidx + 1) * 128] = jnp.where(low, lo, hi).astype(o_ref.dtype)


def swa_attn_sample(hrow, q_s, k_cache, v_cache, k_new, v_new, past_len, group, scale, seqs=4):
    db, tq, d = q_s.shape
    kvw = k_cache.shape[2]
    seq = lambda r, w: pl.BlockSpec((seqs, r, w), lambda b: (b, 0, 0))
    return pl.pallas_call(
        functools.partial(_swa_sample_kernel, group=group, scale=scale, past_len=past_len),
        grid=(db // seqs,),
        in_specs=[pl.BlockSpec(hrow.shape, lambda b: (0, 0, 0)),
                  seq(tq, d), seq(WINDOW, kvw), seq(tq, kvw), seq(WINDOW, kvw), seq(tq, kvw)],
        out_specs=seq(tq, d),
        out_shape=jax.ShapeDtypeStruct((db, tq, d), BF16),
        compiler_params=_params(("parallel",), 32),
        name="swa_attn_sample",
    )(hrow, q_s, k_cache, k_new, v_cache, v_new)


def _pair_swap(w):
    return w.reshape(w.shape[:-1] + (w.shape[-1] // 2, 2))[..., ::-1].reshape(w.shape)


def _swa_slab_perm(heads, kv_heads, hd):
    group = heads // kv_heads
    cols = []
    for p in range(kv_heads // 2):
        for i in range(group):
            for half in range(2):
                head = (2 * p + half) * group + i
                cols.extend(range(head * hd, (head + 1) * hd))
    return np.asarray(cols, np.int32)


def kernel(x_prompt, x_sample, cache_mla_ckv, cache_mla_kpe, cache_swa_k, cache_swa_v, page_table, meta_tokens, w_dq, g_q, w_uq, w_dkv, g_kv, w_uk, w_uv, w_o_mla, w_k_shared, w_v_shared, w_q_swa, w_o_swa, sinks, ln_mix_g, ln_mix_b, ln_ffn_g, ln_ffn_b, w_gate, w_up, w_down, w_router, we_gate, we_up, we_down):
    b, seq, d = x_prompt.shape
    db, ds, _ = x_sample.shape
    depth = ln_mix_g.shape[0]
    n_a = w_dq.shape[0]
    n_meta = meta_tokens.shape[0]
    kv_lora, heads, qk_nope = w_uk.shape[1:]
    qk_rope = cache_mla_kpe.shape[3]
    v_head = w_uv.shape[3]
    swa_kv_heads, swa_hd = cache_swa_k.shape[2:]
    swa_heads = w_q_swa.shape[2] // swa_hd
    swa_group = swa_heads // swa_kv_heads
    win_buf = cache_swa_k.shape[1]
    past_len = page_table.shape[1] * cache_mla_ckv.shape[2]
    pad = BLOCK - n_meta
    lp = seq + BLOCK
    n_p = b * lp
    n_s = db * ds
    n = n_p + n_s
    alpha = (2 * depth) ** 0.25
    mla_scale = (qk_nope + qk_rope) ** -0.5
    swa_scale = swa_hd ** -0.5
    assert (qk_nope, qk_rope, v_head, swa_hd, win_buf) == (128, 64, 128, 64, WINDOW)
    assert n_p % 512 == 0 and n_s == 512 and n % 1024 == 0
    sample_block = n_p // n_s
    ds_pad = 16

    front = jnp.concatenate([jnp.zeros((pad, d), x_prompt.dtype), meta_tokens.astype(x_prompt.dtype)], 0)
    x = jnp.concatenate([piece for i in range(b) for piece in (front, x_prompt[i])] + [x_sample.reshape(n_s, d)], 0)
    xb = x

    pos = jnp.concatenate([jnp.tile(jnp.arange(lp, dtype=jnp.int32) - pad, b),
                           jnp.tile(past_len + jnp.arange(ds, dtype=jnp.int32), db)])
    inv = ROPE_THETA ** (-jnp.arange(0, qk_rope, 2, dtype=F32) / qk_rope)
    ang = pos.astype(F32)[:, None] * inv
    cos, sin = jnp.cos(ang), jnp.sin(ang)
    ctab = jnp.pad(jnp.repeat(cos, 2, axis=-1), ((0, 0), (0, 128 - qk_rope)))
    stab = jnp.pad(jnp.stack([-sin, sin], -1).reshape(n, qk_rope), ((0, 0), (0, 128 - qk_rope)))

    pool_kpe_t = jnp.swapaxes(cache_mla_kpe, 2, 3)
    slopes = 2.0 ** (-ALIBI_MAX_BIAS * jnp.arange(1, swa_heads + 1, dtype=F32) / swa_heads)
    perm = _swa_slab_perm(swa_heads, swa_kv_heads, swa_hd)
    head_of_block = perm[::swa_hd] // swa_hd

    ckv_out, kpe_out = [], []
    kv = None
    for l in range(depth):
        g_mix, b_mix = ln_mix_g[l][None], ln_mix_b[l][None]
        if l < n_a:
            w_rope = w_dkv[l][:, kv_lora:]
            w_rope_sw = _pair_swap(w_rope)
            wr = jnp.concatenate([w_rope, w_rope_sw, w_rope_sw, w_rope], -1).astype(BF16)
            wq3 = w_uq[l].reshape(-1, heads, qk_nope + qk_rope)
            wq_rope = wq3[..., qk_nope:]
            wq_rope_sw = _pair_swap(wq_rope)
            wq2 = jnp.concatenate([wq3[..., :qk_nope], wq_rope, wq_rope_sw, wq_rope_sw, wq_rope], -1)
            wq2 = wq2.reshape(-1, heads * 384).astype(BF16)
            wk2 = w_uk[l].reshape(kv_lora, heads * qk_nope).astype(BF16)
            wv2 = w_uv[l].reshape(kv_lora, heads * v_head).astype(BF16)

            cq, ckv, ckvb, kpe = mla_down(xb, w_dq[l].astype(BF16), w_dkv[l][:, :kv_lora].astype(BF16), wr,
                                          g_q[l][None], g_kv[l][None], ctab, stab)
            q_full = mla_qup(cq, wq2, ctab, stab, heads)
            k_full, v_full = mla_kvup(ckvb, kpe, wk2, wv2, n_p, heads)
            attn_p = mla_prompt_attn(q_full, k_full, v_full, n_p, b, lp, heads, pad, mla_scale)

            q_lat = mla_absorb(q_full, wk2, sample_block, n_s, heads).reshape(n_s * heads, kv_lora)
            q_s = q_full[n_p:].reshape(n_s * heads, 256)
            ckv_new = jnp.pad(ckvb[n_p:].reshape(db, ds, kv_lora), ((0, 0), (0, ds_pad - ds), (0, 0)))
            kpe_new = jnp.pad(kpe[n_p:].reshape(db, ds, 128), ((0, 0), (0, ds_pad - ds), (0, 0)))
            o_lat = mla_decode(page_table, q_lat, q_s, ckv_new, kpe_new, cache_mla_ckv, pool_kpe_t,
                               l, heads, mla_scale)
            attn_s = mla_vup(o_lat.reshape(n_s, heads * kv_lora), wv2, heads)
            x, xb = proj_ln(attn_p, attn_s, w_o_mla[l].astype(BF16), x, g_mix, b_mix, alpha)
            ckv_out.append(ckv)
            kpe_out.append(kpe[:, :qk_rope])
        else:
            j = l - n_a
            if kv is None:
                w_kv = jnp.concatenate([w_k_shared, w_v_shared], -1).astype(BF16)
                kv = matmul_rows(xb, w_kv, F32)
                kvw = swa_kv_heads * swa_hd
                k_new = kv[n_p:, :kvw].reshape(db, ds, kvw)
                v_new = kv[n_p:, kvw:].reshape(db, ds, kvw)
                k_new_pad = jnp.pad(k_new, ((0, 0), (0, ds_pad - ds), (0, 0)))
                v_new_pad = jnp.pad(v_new, ((0, 0), (0, ds_pad - ds), (0, 0)))
            w_q = w_q_swa[j].reshape(d, swa_kv_heads // 2, 2, swa_group, swa_hd).transpose(0, 1, 3, 2, 4)
            w_o = w_o_swa[j].reshape(swa_kv_heads // 2, 2, swa_group, swa_hd, d).transpose(0, 2, 1, 3, 4)
            q = matmul_rows(xb, w_q.reshape(d, -1).astype(BF16), BF16)
            hs = jnp.stack([slopes, sinks[j].astype(F32)])
            attn_p = swa_attn_prompt(hs, q, kv, n_p, b, lp, pad, swa_group, swa_scale)
            q_s = jnp.pad(q[n_p:].reshape(db, ds, -1), ((0, 0), (0, ds_pad - ds), (0, 0)))
            hrow = jnp.broadcast_to(jnp.repeat(hs[:, head_of_block], ds_pad, axis=1)[:, :, None],
                                    (2, head_of_block.size * ds_pad, 128))
            attn_s = swa_attn_sample(hrow, q_s, cache_swa_k.reshape(db, win_buf, kvw),
                                     cache_swa_v.reshape(db, win_buf, kvw), k_new_pad, v_new_pad,
                                     past_len, swa_group, swa_scale)
            x, xb = proj_ln(attn_p, attn_s[:, :ds].reshape(n_s, -1), w_o.reshape(-1, d).astype(BF16),
                            x, g_mix, b_mix, alpha)

        g_ffn, b_ffn = ln_ffn_g[l][None], ln_ffn_b[l][None]
        f = l // 2
        if l % 2 == 0:
            h = ffn_gate_up(xb, w_gate, w_up, f)
            x, xb = ffn_down_ln(h, w_down, f, x, g_ffn, b_ffn, alpha)
        else:
            x, xb = moe_ffn_ln(x, w_router, we_gate, we_up, we_down, f, g_ffn, b_ffn, alpha)

    y_prompt = jnp.stack([x[i * lp + BLOCK:(i + 1) * lp] for i in range(b)])
    y_sample = x[n_p:].reshape(db, ds, d)
    new_ckv_prompt = jnp.stack([c[:n_p].reshape(b, lp, kv_lora)[:, pad:] for c in ckv_out])
    new_kpe_prompt = jnp.stack([c[:n_p].reshape(b, lp, qk_rope)[:, pad:] for c in kpe_out])
    new_ckv_sample = jnp.stack([c[n_p:].reshape(db, ds, kv_lora) for c in ckv_out])
    new_kpe_sample = jnp.stack([c[n_p:].reshape(db, ds, qk_rope) for c in kpe_out])
    wp = min(WINDOW, seq + n_meta)
    k_p = kv[:n_p, :kvw].reshape(b, lp, swa_kv_heads, swa_hd)
    v_p = kv[:n_p, kvw:].reshape(b, lp, swa_kv_heads, swa_hd)
    new_swa_k_prompt = k_p[:, lp - wp:]
    new_swa_v_prompt = v_p[:, lp - wp:]
    new_swa_k_sample = jnp.concatenate([cache_swa_k, k_new.reshape(db, ds, swa_kv_heads, swa_hd)], 1)[:, ds:]
    new_swa_v_sample = jnp.concatenate([cache_swa_v, v_new.reshape(db, ds, swa_kv_heads, swa_hd)], 1)[:, ds:]
    return (y_prompt, y_sample, new_ckv_prompt, new_kpe_prompt, new_ckv_sample, new_kpe_sample,
            new_swa_k_prompt, new_swa_v_prompt, new_swa_k_sample, new_swa_v_sample)
```
